```python
import math
import jax, jax.numpy as jnp
from jax import lax
import numpy as np

D_MODEL = 1024
BATCH = 2
SEQ = 8192
DEPTH = 1
DEC_BATCH = 32
DEC_SEQ = 16
PAST_LEN = 1024

CHUNK = 64
Q_BLOCK = 128

GLA_HEADS = 4
GLA_DK = (D_MODEL // 2) // GLA_HEADS
GLA_DV = D_MODEL // GLA_HEADS
GLA_RANK = 16
GLA_TAU = 16.0
GLA_BLOCK = 16

DIFF_HEAD_DIM = 64
DIFF_HEADS = D_MODEL // (2 * DIFF_HEAD_DIM)

D_FF = 2816
CONV_W = 3

EPS = 1e-5
DEEPNORM_ALPHA = (2.0 * DEPTH) ** 0.25
DEEPNORM_BETA = (8.0 * DEPTH) ** -0.25

GLA_QK = GLA_HEADS * GLA_DK
GLA_V = GLA_HEADS * GLA_DV
DIFF_QK = DIFF_HEADS * 2 * DIFF_HEAD_DIM
DIFF_V = DIFF_HEADS * 2 * DIFF_HEAD_DIM
IN_SPLITS = (GLA_QK, GLA_QK, GLA_V, GLA_V, GLA_RANK, DIFF_QK, DIFF_QK, DIFF_V, D_MODEL, D_MODEL)
IN_OFFSETS = tuple(int(o) for o in np.cumsum(IN_SPLITS)[:-1])
D_IN = int(sum(IN_SPLITS))

kernel_name = "streaming_gla_diffattn_convffn_deepnorm"


def _layer_norm(x, g, b):
    xf = x.astype(jnp.float32)
    mu = jnp.mean(xf, -1, keepdims=True)
    var = jnp.mean(jnp.square(xf - mu), -1, keepdims=True)
    return ((xf - mu) * lax.rsqrt(var + EPS) * g.astype(jnp.float32) + b.astype(jnp.float32)).astype(x.dtype)


def _rms_norm(x, g):
    xf = x.astype(jnp.float32)
    return xf * lax.rsqrt(jnp.mean(xf * xf, -1, keepdims=True) + EPS) * g.astype(jnp.float32)


def _gla_chunked(q, k, v, log_a, s0):
    b, t, h, _ = q.shape
    dv = v.shape[-1]
    n = t // GLA_BLOCK

    def blocks(z):
        return jnp.moveaxis(z.astype(jnp.float32).reshape(b, n, GLA_BLOCK, h, z.shape[-1]), 1, 0)

    causal = jnp.tril(jnp.ones((GLA_BLOCK, GLA_BLOCK), bool))

    def step(s, blk):
        qb, kb, vb, lb = blk
        cum = jnp.cumsum(lb, axis=1)
        q_dec = qb * jnp.exp(cum)
        k_inv = kb * jnp.exp(-cum)
        att = jnp.einsum('bthk,bshk->bhts', q_dec, k_inv)
        att = jnp.where(causal, att, 0.0)
        o = jnp.einsum('bhts,bshv->bthv', att, vb) + jnp.einsum('bthk,bhkv->bthv', q_dec, s)
        last = cum[:, -1]
        k_end = kb * jnp.exp(last[:, None] - cum)
        s = jnp.exp(last)[..., None] * s + jnp.einsum('bshk,bshv->bhkv', k_end, vb)
        return s, o

    s_fin, o = lax.scan(step, s0.astype(jnp.float32), (blocks(q), blocks(k), blocks(v), blocks(log_a)))
    o = jnp.moveaxis(o, 0, 1).reshape(b, t, h, dv)
    return o, s_fin


def _diff_attend(q, k, v, mask, lam):
    s = jnp.einsum('bqhme,bkhme->bhmqk', q, k).astype(jnp.float32) * (DIFF_HEAD_DIM ** -0.5)
    s = jnp.where(mask, s, -jnp.inf)
    p = jax.nn.softmax(s, axis=-1)
    w = p[:, :, 0] - lam * p[:, :, 1]
    return jnp.einsum('bhqk,bkhe->bqhe', w.astype(v.dtype), v)


def _chunk_mask(q_pos, key_pos):
    return key_pos[None, :] < (q_pos[:, None] // CHUNK + 1) * CHUNK


def _diff_attn_prompt(q, k, v, lam):
    b, t = q.shape[:2]
    nb = t // Q_BLOCK
    q_blocks = jnp.moveaxis(q.reshape(b, nb, Q_BLOCK, DIFF_HEADS, 2, DIFF_HEAD_DIM), 1, 0)
    key_pos = jnp.arange(t)

    def one(args):
        qb, i = args
        q_pos = i * Q_BLOCK + jnp.arange(Q_BLOCK)
        return _diff_attend(qb, k, v, _chunk_mask(q_pos, key_pos), lam)

    o = lax.map(one, (q_blocks, jnp.arange(nb)))
    return jnp.moveaxis(o, 0, 1).reshape(b, t, DIFF_HEADS, 2 * DIFF_HEAD_DIM)


def _layer(x, past_k, past_v, gla_s0, ffn_past, weights, layer_idx):
    (w_in, w_a2, b_a, gla_norm_g, lam_q1, lam_k1, lam_q2, lam_k2, diff_norm_g,
     w_out, ln1_g, ln1_b, w_up, conv_w, conv_b, w_down, ln2_g, ln2_b) = weights
    bsz, t, _ = x.shape
    is_prompt = past_k is None

    proj = x @ w_in
    gq, gk, gv, gg, g_low, dq, dk, dv, gate_a, gate_b = jnp.split(proj, IN_OFFSETS, axis=-1)

    log_a = (jax.nn.log_sigmoid((g_low @ w_a2 + b_a).astype(jnp.float32)) / GLA_TAU).reshape(bsz, t, GLA_HEADS, GLA_DK)
    q = gq.reshape(bsz, t, GLA_HEADS, GLA_DK) * (GLA_DK ** -0.5)
    k = gk.reshape(bsz, t, GLA_HEADS, GLA_DK)
    v = gv.reshape(bsz, t, GLA_HEADS, GLA_DV)
    pad = (-t) % GLA_BLOCK
    if pad:
        pw = ((0, 0), (0, pad), (0, 0), (0, 0))
        q, k, v, log_a = jnp.pad(q, pw), jnp.pad(k, pw), jnp.pad(v, pw), jnp.pad(log_a, pw)
    s_dtype = x.dtype if is_prompt else gla_s0.dtype
    s0 = jnp.zeros((bsz, GLA_HEADS, GLA_DK, GLA_DV), jnp.float32) if is_prompt else gla_s0
    o_g, s_new = _gla_chunked(q, k, v, log_a, s0)
    o_g = _rms_norm(o_g[:, :t], gla_norm_g) * jax.nn.silu(gg.reshape(bsz, t, GLA_HEADS, GLA_DV).astype(jnp.float32))
    o_gla = o_g.reshape(bsz, t, GLA_V).astype(x.dtype)

    qd = dq.reshape(bsz, t, DIFF_HEADS, 2, DIFF_HEAD_DIM)
    kd = dk.reshape(bsz, t, DIFF_HEADS, 2, DIFF_HEAD_DIM)
    vd = dv.reshape(bsz, t, DIFF_HEADS, 2 * DIFF_HEAD_DIM)
    lam_init = 0.8 - 0.6 * math.exp(-0.3 * layer_idx)
    lam = (jnp.exp(jnp.sum(lam_q1.astype(jnp.float32) * lam_k1.astype(jnp.float32)))
           - jnp.exp(jnp.sum(lam_q2.astype(jnp.float32) * lam_k2.astype(jnp.float32))) + lam_init)
    if is_prompt:
        o_d = _diff_attn_prompt(qd, kd, vd, lam)
    else:
        past_len = past_k.shape[1]
        k_all = jnp.concatenate([past_k, kd.astype(past_k.dtype)], axis=1)
        v_all = jnp.concatenate([past_v, vd.astype(past_v.dtype)], axis=1)
        mask = _chunk_mask(past_len + jnp.arange(t), jnp.arange(past_len + t))
        o_d = _diff_attend(qd, k_all, v_all, mask, lam)
    o_diff = (_rms_norm(o_d, diff_norm_g) * (1.0 - lam_init)).reshape(bsz, t, DIFF_V).astype(x.dtype)

    mixed = jax.nn.sigmoid(gate_a) * o_gla + jax.nn.sigmoid(gate_b) * o_diff
    x = _layer_norm(DEEPNORM_ALPHA * x + mixed @ w_out, ln1_g, ln1_b)

    up = x @ w_up
    a, gate = jnp.split(up, [D_FF], axis=-1)
    left = jnp.zeros((bsz, CONV_W - 1, D_FF), a.dtype) if is_prompt else ffn_past.astype(a.dtype)
    a_full = jnp.concatenate([left, a], axis=1)
    a_conv = conv_b
    for j in range(CONV_W):
        a_conv = a_conv + a_full[:, j:j + t] * conv_w[j]
    h = jax.nn.gelu(a_conv) * gate
    x = _layer_norm(DEEPNORM_ALPHA * x + h @ w_down, ln2_g, ln2_b)

    ffn_new = a_full[:, -(CONV_W - 1):]
    return x, kd, vd, s_new.astype(s_dtype), ffn_new


def setup_inputs(seed: int = 0) -> dict:
    key = jax.random.key(seed)
    ks = jax.random.split(key, 26)
    nrm = lambda i, shape: jax.random.normal(ks[i], shape, jnp.float32)
    return {
        "x_prompt": nrm(0, (BATCH, SEQ, D_MODEL)),
        "x_sample": nrm(1, (DEC_BATCH, DEC_SEQ, D_MODEL)),
        "cache_diff_k": nrm(2, (DEPTH, DEC_BATCH, PAST_LEN, DIFF_HEADS, 2, DIFF_HEAD_DIM)),
        "cache_diff_v": nrm(3, (DEPTH, DEC_BATCH, PAST_LEN, DIFF_HEADS, 2 * DIFF_HEAD_DIM)),
        "state_gla": 0.5 * nrm(4, (DEPTH, DEC_BATCH, GLA_HEADS, GLA_DK, GLA_DV)),
        "cache_ffn_conv": nrm(5, (DEPTH, DEC_BATCH, CONV_W - 1, D_FF)),
        "w_in": nrm(6, (DEPTH, D_MODEL, D_IN)) * D_MODEL ** -0.5,
        "w_a2": nrm(7, (DEPTH, GLA_RANK, GLA_QK)) * GLA_RANK ** -0.5,
        "b_a": 0.1 * nrm(8, (DEPTH, GLA_QK)),
        "gla_norm_g": 1.0 + 0.02 * nrm(9, (DEPTH, GLA_DV)),
        "lam_q1": 0.1 * nrm(10, (DEPTH, DIFF_HEAD_DIM)),
        "lam_k1": 0.1 * nrm(11, (DEPTH, DIFF_HEAD_DIM)),
        "lam_q2": 0.1 * nrm(12, (DEPTH, DIFF_HEAD_DIM)),
        "lam_k2": 0.1 * nrm(13, (DEPTH, DIFF_HEAD_DIM)),
        "diff_norm_g": 1.0 + 0.02 * nrm(14, (DEPTH, 2 * DIFF_HEAD_DIM)),
        "w_out": nrm(15, (DEPTH, D_MODEL, D_MODEL)) * (D_MODEL ** -0.5 * DEEPNORM_BETA),
        "ln1_g": 1.0 + 0.02 * nrm(16, (DEPTH, D_MODEL)),
        "ln1_b": 0.02 * nrm(17, (DEPTH, D_MODEL)),
        "w_up": nrm(18, (DEPTH, D_MODEL, 2 * D_FF)) * D_MODEL ** -0.5,
        "conv_w": nrm(19, (DEPTH, CONV_W, D_FF)) * CONV_W ** -0.5,
        "conv_b": 0.02 * nrm(20, (DEPTH, D_FF)),
        "w_down": nrm(21, (DEPTH, D_FF, D_MODEL)) * (D_FF ** -0.5 * DEEPNORM_BETA),
        "ln2_g": 1.0 + 0.02 * nrm(22, (DEPTH, D_MODEL)),
        "ln2_b": 0.02 * nrm(23, (DEPTH, D_MODEL)),
    }


def reference(x_prompt, x_sample, cache_diff_k, cache_diff_v, state_gla, cache_ffn_conv,
              w_in, w_a2, b_a, gla_norm_g, lam_q1, lam_k1, lam_q2, lam_k2, diff_norm_g,
              w_out, ln1_g, ln1_b, w_up, conv_w, conv_b, w_down, ln2_g, ln2_b):
    yp, ys = x_prompt, x_sample
    pk, pv, ps, pc, sk, sv, ss, sc = [], [], [], [], [], [], [], []
    for l in range(DEPTH):
        weights = (w_in[l], w_a2[l], b_a[l], gla_norm_g[l], lam_q1[l], lam_k1[l], lam_q2[l], lam_k2[l],
                   diff_norm_g[l], w_out[l], ln1_g[l], ln1_b[l], w_up[l], conv_w[l], conv_b[l],
                   w_down[l], ln2_g[l], ln2_b[l])
        yp, k_new, v_new, s_new, c_new = _layer(yp, None, None, None, None, weights, l)
        pk.append(k_new)
        pv.append(v_new)
        ps.append(s_new)
        pc.append(c_new)
        ys, k_new, v_new, s_new, c_new = _layer(ys, cache_diff_k[l], cache_diff_v[l], state_gla[l],
                                                cache_ffn_conv[l], weights, l)
        sk.append(k_new)
        sv.append(v_new)
        ss.append(s_new)
        sc.append(c_new)
    return (yp, ys, jnp.stack(pk), jnp.stack(pv), jnp.stack(ps), jnp.stack(pc),
            jnp.stack(sk), jnp.stack(sv), jnp.stack(ss), jnp.stack(sc))
```

```python
import functools
import math

import numpy as np
import jax
import jax.numpy as jnp
from jax import lax
from jax.experimental import pallas as pl
from jax.experimental.pallas import tpu as pltpu

F32 = jnp.float32
BF16 = jnp.bfloat16

CHUNK = 64
GLA_HEADS = 4
GLA_RANK = 16
GLA_TAU = 16.0
GLA_BLOCK = 16
DIFF_HEAD_DIM = 64
CONV_W = 3
EPS = 1e-5

LANES = 128
SUBLANES = 8
VMEM_LIMIT = 56 * 1024 * 1024

NT_DIMS = (((1,), (1,)), ((), ()))
TN_DIMS = (((0,), (0,)), ((), ()))


def _sigmoid(x):
    return 1.0 / (1.0 + jnp.exp(-x))


def _layer_norm(x, g, b):
    mu = jnp.mean(x, -1, keepdims=True)
    xc = x - mu
    var = jnp.mean(xc * xc, -1, keepdims=True)
    return xc * lax.rsqrt(var + EPS) * g + b


def _resident(shape):
    nd = len(shape)
    return pl.BlockSpec(shape, lambda *_: (0,) * nd, pipeline_mode=pl.Buffered(1))


def _params(sem):
    return pltpu.CompilerParams(dimension_semantics=sem, vmem_limit_bytes=VMEM_LIMIT)


def _proj_kernel(x_ref, wgla_ref, wlow_ref, wa2_ref, ba_ref, wdiff_ref,
                 gq_ref, gk_ref, gv_ref, gg_ref, la_ref, dq_ref, dk_ref, dv_ref, kb_ref, vb_ref,
                 *, qk, vd, dqk):
    xb = x_ref[...].astype(BF16)

    def mm(w_ref, lo, hi):
        return jnp.dot(xb, w_ref[:, lo:hi], preferred_element_type=F32)

    gq_ref[...] = mm(wgla_ref, 0, qk)
    gk_ref[...] = mm(wgla_ref, qk, 2 * qk)
    gv_ref[...] = mm(wgla_ref, 2 * qk, 2 * qk + vd)
    gg_ref[...] = mm(wgla_ref, 2 * qk + vd, 2 * qk + 2 * vd)

    g_low = jnp.dot(xb, wlow_ref[...], preferred_element_type=F32)
    z = jnp.dot(g_low.astype(BF16), wa2_ref[...], preferred_element_type=F32) + ba_ref[...]
    softplus_neg = jnp.maximum(-z, 0.0) + jnp.log1p(jnp.exp(-jnp.abs(z)))
    la_ref[...] = -softplus_neg * (1.0 / GLA_TAU)

    dq_ref[...] = (mm(wdiff_ref, 0, dqk) * (DIFF_HEAD_DIM ** -0.5)).astype(BF16)
    dk = mm(wdiff_ref, dqk, 2 * dqk)
    dk_ref[...] = dk
    kb_ref[...] = dk.astype(BF16)
    dv = mm(wdiff_ref, 2 * dqk, 3 * dqk)
    dv_ref[...] = dv
    vb_ref[...] = dv.astype(BF16)


def _proj(x, w_gla, w_low, w_a2, b_a, w_diff, *, qk, vd, dqk, tm):
    rows, d = x.shape
    tm = min(tm, rows)
    row_spec = lambda n: pl.BlockSpec((tm, n), lambda i: (i, 0))
    out = lambda n, dt: jax.ShapeDtypeStruct((rows, n), dt)
    return pl.pallas_call(
        functools.partial(_proj_kernel, qk=qk, vd=vd, dqk=dqk),
        grid=(rows // tm,),
        in_specs=[row_spec(d), _resident(w_gla.shape), _resident(w_low.shape), _resident(w_a2.shape),
                  _resident(b_a.shape), _resident(w_diff.shape)],
        out_specs=[row_spec(qk), row_spec(qk), row_spec(vd), row_spec(vd), row_spec(qk),
                   row_spec(dqk), row_spec(dqk), row_spec(dqk), row_spec(dqk), row_spec(dqk)],
        out_shape=[out(qk, F32), out(qk, F32), out(vd, F32), out(vd, F32), out(qk, F32),
                   out(dqk, BF16), out(dqk, F32), out(dqk, F32), out(dqk, BF16), out(dqk, BF16)],
        compiler_params=_params(("parallel",)),
        name="proj",
    )(x, w_gla, w_low, w_a2, b_a, w_diff)


def _gla_levels(c):
    d = min(c, 2 * GLA_BLOCK)
    off = []
    b = c // 2
    while b >= d:
        off.append(b)
        b //= 2
    return off, d


def _row_broadcast_refs(cum, group, ref_offset):
    c, n = cum.shape
    parts = [jnp.broadcast_to(cum[p * group + ref_offset:p * group + ref_offset + 1, :], (group, n))
             for p in range(c // group)]
    return parts[0] if len(parts) == 1 else jnp.concatenate(parts, axis=0)


def _gla_kernel(*refs, c, dk, dv, has_s0):
    if has_s0:
        q_ref, k_ref, v_ref, gg_ref, la_ref, gn_ref, s0_ref, o_ref, sout_ref, s_scr = refs
    else:
        q_ref, k_ref, v_ref, gg_ref, la_ref, gn_ref, o_ref, sout_ref, s_scr = refs
    step = pl.program_id(1)

    @pl.when(step == 0)
    def _():
        s_scr[...] = s0_ref[0] if has_s0 else jnp.zeros_like(s_scr)

    row = lax.broadcasted_iota(jnp.int32, (c, c), 0)
    col = lax.broadcasted_iota(jnp.int32, (c, c), 1)
    tri = (col <= row).astype(F32)
    cum_all = jnp.dot(tri, la_ref[...], precision=lax.Precision.HIGHEST, preferred_element_type=F32)

    off_levels, d = _gla_levels(c)
    scale = dk ** -0.5
    gn = gn_ref[...]

    for h in range(GLA_HEADS):
        ks = slice(h * dk, (h + 1) * dk)
        vs = slice(h * dv, (h + 1) * dv)
        cum = cum_all[:, ks]
        qs = q_ref[:, ks] * scale
        kk = k_ref[:, ks]
        vv = v_ref[:, vs].astype(BF16)

        ref = _row_broadcast_refs(cum, d, d // 2 - 1)
        qd = (qs * jnp.exp(cum - ref)).astype(BF16)
        kd = (kk * jnp.exp(ref - cum)).astype(BF16)
        att = lax.dot_general(qd, kd, NT_DIMS, preferred_element_type=F32)
        att = jnp.where((row // d == col // d) & (col <= row), att, 0.0)
        for b in off_levels:
            ref = _row_broadcast_refs(cum, 2 * b, b - 1)
            ql = (qs * jnp.exp(jnp.minimum(cum - ref, 0.0))).astype(BF16)
            kl = (kk * jnp.exp(jnp.minimum(ref - cum, 0.0))).astype(BF16)
            a_l = lax.dot_general(ql, kl, NT_DIMS, preferred_element_type=F32)
            att = jnp.where(((row // b) % 2 == 1) & (col // b == row // b - 1), a_l, att)

        s_h = s_scr[h]
        q_dec = (qs * jnp.exp(cum)).astype(BF16)
        o = (jnp.dot(att.astype(BF16), vv, preferred_element_type=F32)
             + jnp.dot(q_dec, s_h.astype(BF16), preferred_element_type=F32))

        last = cum[c - 1:c, :]
        k_end = (kk * jnp.exp(last - cum)).astype(BF16)
        kv = lax.dot_general(k_end, vv, TN_DIMS, preferred_element_type=F32)
        decay_col = jnp.transpose(jnp.broadcast_to(jnp.exp(last), (dk, dk)))
        decay = jnp.concatenate([decay_col] * (dv // dk), axis=1)
        s_scr[h] = decay * s_h + kv

        ms = jnp.mean(o * o, -1, keepdims=True)
        gate = gg_ref[:, vs]
        o_ref[:, vs] = o * lax.rsqrt(ms + EPS) * gn * (gate * _sigmoid(gate))

    @pl.when(step == pl.num_programs(1) - 1)
    def _():
        sout_ref[0] = s_scr[...]


def _gla(gq, gk, gv, gg, la, gn, s0, *, nb, t, c):
    rows, qk = gq.shape
    vd = gv.shape[1]
    dk, dv = qk // GLA_HEADS, vd // GLA_HEADS
    c = min(c, t)
    nc = t // c
    blk = lambda n: pl.BlockSpec((c, n), lambda b, s: (b * nc + s, 0))
    state_spec = pl.BlockSpec((1, GLA_HEADS, dk, dv), lambda b, s: (b, 0, 0, 0))
    in_specs = [blk(qk), blk(qk), blk(vd), blk(vd), blk(qk), _resident(gn.shape)]
    args = [gq, gk, gv, gg, la, gn]
    if s0 is not None:
        in_specs.append(state_spec)
        args.append(s0)
    return pl.pallas_call(
        functools.partial(_gla_kernel, c=c, dk=dk, dv=dv, has_s0=s0 is not None),
        grid=(nb, nc),
        in_specs=in_specs,
        out_specs=[blk(vd), state_spec],
        out_shape=[jax.ShapeDtypeStruct((rows, vd), F32),
                   jax.ShapeDtypeStruct((nb, GLA_HEADS, dk, dv), F32)],
        scratch_shapes=[pltpu.VMEM((GLA_HEADS, dk, dv), F32)],
        compiler_params=_params(("parallel", "arbitrary")),
        name="gla",
    )(*args)


def _stack_half_queries(q):
    lane = lax.broadcasted_iota(jnp.int32, q.shape, 1)
    zero = jnp.zeros_like(q)
    return jnp.concatenate([jnp.where(lane < DIFF_HEAD_DIM, q, zero),
                            jnp.where(lane >= DIFF_HEAD_DIM, q, zero)], axis=0)


def _lambda(lq1_ref, lk1_ref, lq2_ref, lk2_ref, lam_init):
    return (jnp.exp(jnp.sum(lq1_ref[...] * lk1_ref[...], -1, keepdims=True))
            - jnp.exp(jnp.sum(lq2_ref[...] * lk2_ref[...], -1, keepdims=True)) + lam_init)


def _diff_finish(acc, l, lam, g, lam_init, tq):
    o = acc * (1.0 / l)
    od = o[:tq] - lam * o[tq:]
    ms = jnp.mean(od * od, -1, keepdims=True)
    return od * lax.rsqrt(ms + EPS) * g * (1.0 - lam_init)


def _attn_prompt_kernel(q_ref, k_ref, v_ref, lq1_ref, lk1_ref, lq2_ref, lk2_ref, g_ref, o_ref,
                        *, tq, tk, lam_init):
    i = pl.program_id(2)
    qs = _stack_half_queries(q_ref[...])
    q_first = i * tq

    def tile(j, carry, masked):
        m, l, acc = carry
        start = pl.multiple_of(j * tk, tk)
        k = k_ref[pl.ds(start, tk), :]
        v = v_ref[pl.ds(start, tk), :]
        s = lax.dot_general(qs, k, NT_DIMS, preferred_element_type=F32)
        if masked:
            r = lax.broadcasted_iota(jnp.int32, s.shape, 0)
            q_pos = q_first + jnp.where(r >= tq, r - tq, r)
            k_pos = start + lax.broadcasted_iota(jnp.int32, s.shape, 1)
            s = jnp.where(k_pos < (q_pos // CHUNK + 1) * CHUNK, s, -jnp.inf)
        m_new = jnp.maximum(m, jnp.max(s, -1, keepdims=True))
        alpha = jnp.exp(m - m_new)
        p = jnp.exp(s - m_new)
        l = alpha * l + jnp.sum(p, -1, keepdims=True)
        acc = alpha * acc + jnp.dot(p.astype(BF16), v, preferred_element_type=F32)
        return m_new, l, acc

    n_full = (q_first + CHUNK) // tk
    n_any = (q_first + tq + tk - 1) // tk
    init = (jnp.full((2 * tq, 1), -jnp.inf, F32), jnp.zeros((2 * tq, 1), F32),
            jnp.zeros((2 * tq, v_ref.shape[1]), F32))
    carry = lax.fori_loop(0, n_full, functools.partial(tile, masked=False), init)
    _, l, acc = lax.fori_loop(n_full, n_any, functools.partial(tile, masked=True), carry)

    lam = _lambda(lq1_ref, lk1_ref, lq2_ref, lk2_ref, lam_init)
    o_ref[...] = _diff_finish(acc, l, lam, g_ref[...], lam_init, tq)


def _attn_prompt(dq, kb, vb, lams, g, *, nb, t, tq, tk, lam_init):
    rows, dqk = dq.shape
    hw = 2 * DIFF_HEAD_DIM
    heads = dqk // hw
    tq, tk = min(tq, t), min(tk, t)
    nq = t // tq
    kv_spec = pl.BlockSpec((t, hw), lambda b, h, i: (b, h))
    q_spec = pl.BlockSpec((tq, hw), lambda b, h, i: (b * nq + i, h))
    small = [_resident(a.shape) for a in (*lams, g)]
    return pl.pallas_call(
        functools.partial(_attn_prompt_kernel, tq=tq, tk=tk, lam_init=lam_init),
        grid=(nb, heads, nq),
        in_specs=[q_spec, kv_spec, kv_spec, *small],
        out_specs=q_spec,
        out_shape=jax.ShapeDtypeStruct((rows, dqk), F32),
        compiler_params=_params(("parallel", "parallel", "arbitrary")),
        name="attn_prompt",
    )(dq, kb, vb, *lams, g)


def _attn_sample_kernel(q_ref, kn_ref, vn_ref, kc_ref, vc_ref, lq1_ref, lk1_ref, lq2_ref, lk2_ref, g_ref,
                        o_ref, *, heads, lam_init):
    tq = q_ref.shape[0]
    hw = 2 * DIFF_HEAD_DIM
    lam = _lambda(lq1_ref, lk1_ref, lq2_ref, lk2_ref, lam_init)
    g = g_ref[...]
    for h in range(heads):
        hs = slice(h * hw, (h + 1) * hw)
        qs = _stack_half_queries(q_ref[:, hs])
        k_past = kc_ref[0, :, hs].astype(BF16)
        v_past = vc_ref[0, :, hs].astype(BF16)
        s_past = lax.dot_general(qs, k_past, NT_DIMS, preferred_element_type=F32)
        s_new = lax.dot_general(qs, kn_ref[:, hs], NT_DIMS, preferred_element_type=F32)
        m = jnp.maximum(jnp.max(s_past, -1, keepdims=True), jnp.max(s_new, -1, keepdims=True))
        p_past = jnp.exp(s_past - m)
        p_new = jnp.exp(s_new - m)
        l = jnp.sum(p_past, -1, keepdims=True) + jnp.sum(p_new, -1, keepdims=True)
        acc = (jnp.dot(p_past.astype(BF16), v_past, preferred_element_type=F32)
               + jnp.dot(p_new.astype(BF16), vn_ref[:, hs], preferred_element_type=F32))
        o_ref[:, hs] = _diff_finish(acc, l, lam, g, lam_init, tq)


def _attn_sample(dq, kb, vb, k_cache, v_cache, lams, g, *, nb, t, lam_init):
    rows, dqk = dq.shape
    past = k_cache.shape[1]
    q_pos = past + np.arange(t)
    assert ((q_pos // CHUNK + 1) * CHUNK >= past + t).all(), "sample queries must see every cached and new key"
    new_spec = pl.BlockSpec((t, dqk), lambda b: (b, 0))
    cache_spec = pl.BlockSpec((1, past, dqk), lambda b: (b, 0, 0))
    small = [_resident(a.shape) for a in (*lams, g)]
    return pl.pallas_call(
        functools.partial(_attn_sample_kernel, heads=dqk // (2 * DIFF_HEAD_DIM), lam_init=lam_init),
        grid=(nb,),
        in_specs=[new_spec, new_spec, new_spec, cache_spec, cache_spec, *small],
        out_specs=new_spec,
        out_shape=jax.ShapeDtypeStruct((rows, dqk), F32),
        compiler_params=_params(("parallel",)),
        name="attn_sample",
    )(dq, kb, vb, k_cache, v_cache, *lams, g)


def _gelu_tanh(x):
    return x * (0.5 * (1.0 + jnp.tanh(math.sqrt(2.0 / math.pi) * (x + 0.044715 * (x * x * x)))))


def _ffn_kernel(*refs, tm, d_ff, fc, alpha, blocks_per_seq, seg):
    sample = seg is not None
    if sample:
        (x_ref, og_ref, od_ref, wg_ref, wo_ref, l1g_ref, l1b_ref, wup_ref, cw_ref, cb_ref, wdn_ref,
         l2g_ref, l2b_ref, past_ref, sel1_ref, sel2_ref, y_ref, a_ref, a_scr) = refs
    else:
        (x_ref, og_ref, od_ref, wg_ref, wo_ref, l1g_ref, l1b_ref, wup_ref, cw_ref, cb_ref, wdn_ref,
         l2g_ref, l2b_ref, y_ref, a_ref, a_scr, carry_scr) = refs
    d = x_ref.shape[1]
    x = x_ref[...]
    xb = x.astype(BF16)
    gate_a = jnp.dot(xb, wg_ref[:, 0:d], preferred_element_type=F32)
    gate_b = jnp.dot(xb, wg_ref[:, d:2 * d], preferred_element_type=F32)
    mixed = _sigmoid(gate_a) * og_ref[...] + _sigmoid(gate_b) * od_ref[...]
    x1 = _layer_norm(alpha * x + jnp.dot(mixed.astype(BF16), wo_ref[...], preferred_element_type=F32),
                     l1g_ref[...], l1b_ref[...])
    x1b = x1.astype(BF16)

    pad = SUBLANES
    if sample:
        r = lax.broadcasted_iota(jnp.int32, (tm, fc), 0) % seg
    else:
        first = pl.program_id(0) % blocks_per_seq == 0

    acc = jnp.zeros((tm, d), F32)
    for c in range(d_ff // fc):
        cs = slice(c * fc, (c + 1) * fc)
        a = jnp.dot(x1b, wup_ref[:, cs], preferred_element_type=F32)
        gate = jnp.dot(x1b, wup_ref[:, d_ff + c * fc:d_ff + (c + 1) * fc], preferred_element_type=F32)
        a_scr[pad:pad + tm, :] = a
        if sample:
            a_scr[0:pad, :] = jnp.zeros((pad, fc), F32)
            past = past_ref[:, cs]
            hist1 = jnp.dot(sel1_ref[...], past, precision=lax.Precision.HIGHEST, preferred_element_type=F32)
            hist2 = jnp.dot(sel2_ref[...], past, precision=lax.Precision.HIGHEST, preferred_element_type=F32)
            a1 = jnp.where(r == 0, 0.0, a_scr[pad - 1:pad - 1 + tm, :]) + hist1
            a2 = jnp.where(r < 2, 0.0, a_scr[pad - 2:pad - 2 + tm, :]) + hist2
            a_ref[:, cs] = a
        else:
            a_scr[0:pad, :] = jnp.where(first, 0.0, carry_scr[:, cs])
            a1 = a_scr[pad - 1:pad - 1 + tm, :]
            a2 = a_scr[pad - 2:pad - 2 + tm, :]
            carry_scr[:, cs] = a[tm - pad:tm, :]
            a_ref[0, :, cs] = a[tm - pad:tm, :]
        conv = cb_ref[:, cs] + a2 * cw_ref[0:1, cs] + a1 * cw_ref[1:2, cs] + a * cw_ref[2:3, cs]
        hidden = _gelu_tanh(conv) * gate
        acc = acc + jnp.dot(hidden.astype(BF16), wdn_ref[cs, :], preferred_element_type=F32)

    y_ref[...] = _layer_norm(alpha * x1 + acc, l2g_ref[...], l2b_ref[...])


def _ffn(x, og, od, w_gate, w_out, l1g, l1b, w_up, conv_w, conv_b, w_down, l2g, l2b,
         *, tm, alpha, seq_len, past=None, fc=256):
    rows, d = x.shape
    d_ff = w_down.shape[0]
    tm = min(tm, rows)
    nblk = rows // tm
    sample = past is not None
    row_spec = pl.BlockSpec((tm, d), lambda i: (i, 0))
    weights = [w_gate, w_out, l1g, l1b, w_up, conv_w, conv_b, w_down, l2g, l2b]
    args = [x, og, od, *weights]
    scratch = [pltpu.VMEM((tm + SUBLANES, fc), F32)]
    if sample:
        nseq = tm // seq_len
        sel1 = np.zeros((tm, past.shape[0]), np.float32)
        sel2 = np.zeros((tm, past.shape[0]), np.float32)
        for s in range(nseq):
            sel1[s * seq_len, 2 * s + 1] = 1.0
            sel2[s * seq_len, 2 * s] = 1.0
            sel2[s * seq_len + 1, 2 * s + 1] = 1.0
        assert nblk == 1
        args += [past, jnp.asarray(sel1), jnp.asarray(sel2)]
        a_shape = jax.ShapeDtypeStruct((rows, d_ff), F32)
        a_spec = pl.BlockSpec((tm, d_ff), lambda i: (i, 0))
        kern = functools.partial(_ffn_kernel, tm=tm, d_ff=d_ff, fc=fc, alpha=alpha, blocks_per_seq=None, seg=seq_len)
    else:
        a_shape = jax.ShapeDtypeStruct((nblk, SUBLANES, d_ff), F32)
        a_spec = pl.BlockSpec((1, SUBLANES, d_ff), lambda i: (i, 0, 0))
        scratch.append(pltpu.VMEM((SUBLANES, d_ff), F32))
        kern = functools.partial(_ffn_kernel, tm=tm, d_ff=d_ff, fc=fc, alpha=alpha,
                                 blocks_per_seq=seq_len // tm, seg=None)
    in_specs = [row_spec, row_spec, row_spec] + [_resident(a.shape) for a in args[3:]]
    return pl.pallas_call(
        kern,
        grid=(nblk,),
        in_specs=in_specs,
        out_specs=[row_spec, a_spec],
        out_shape=[jax.ShapeDtypeStruct((rows, d), F32), a_shape],
        scratch_shapes=scratch,
        compiler_params=_params(("arbitrary",)),
        name="ffn",
    )(*args)


def kernel(x_prompt, x_sample, cache_diff_k, cache_diff_v, state_gla, cache_ffn_conv, w_in, w_a2, b_a, gla_norm_g, lam_q1, lam_k1, lam_q2, lam_k2, diff_norm_g, w_out, ln1_g, ln1_b, w_up, conv_w, conv_b, w_down, ln2_g, ln2_b):
    depth = w_in.shape[0]
    assert depth == 1, "single-layer step"
    nbp, tp, d = x_prompt.shape
    nbs, ts, _ = x_sample.shape
    past_len = cache_diff_k.shape[2]
    qk = w_a2.shape[2]
    vd = GLA_HEADS * gla_norm_g.shape[1]
    dqk = cache_diff_v.shape[3] * cache_diff_v.shape[4]
    d_ff = w_down.shape[1]
    alpha = (2.0 * depth) ** 0.25
    lam_init = 0.8 - 0.6 * math.exp(-0.3 * 0)

    w = w_in[0]
    o_low = 2 * qk + 2 * vd
    o_diff = o_low + GLA_RANK
    o_gate = o_diff + 3 * dqk
    w_gla = w[:, :o_low].astype(BF16)
    w_low = jnp.pad(w[:, o_low:o_diff], ((0, 0), (0, LANES - GLA_RANK))).astype(BF16)
    w_diff = w[:, o_diff:o_gate].astype(BF16)
    w_gate = w[:, o_gate:].astype(BF16)
    w_a2p = jnp.pad(w_a2[0], ((0, LANES - GLA_RANK), (0, 0))).astype(BF16)
    row2 = lambda a: a.reshape(1, -1)
    lams = [row2(lam_q1[0]), row2(lam_k1[0]), row2(lam_q2[0]), row2(lam_k2[0])]
    ffn_w = (w_gate, w_out[0].astype(BF16), row2(ln1_g[0]), row2(ln1_b[0]), w_up[0].astype(BF16), conv_w[0],
             row2(conv_b[0]), w_down[0].astype(BF16), row2(ln2_g[0]), row2(ln2_b[0]))
    gn = row2(gla_norm_g[0])
    dg = row2(diff_norm_g[0])
    proj = functools.partial(_proj, w_gla=w_gla, w_low=w_low, w_a2=w_a2p, b_a=row2(b_a[0]), w_diff=w_diff,
                             qk=qk, vd=vd, dqk=dqk, tm=256)

    xp = x_prompt.reshape(nbp * tp, d)
    gq, gk, gv, gg, la, dq, dk_p, dv_p, kb, vb = proj(xp)
    og, state_p = _gla(gq, gk, gv, gg, la, gn, None, nb=nbp, t=tp, c=128)
    od = _attn_prompt(dq, kb, vb, lams, dg, nb=nbp, t=tp, tq=256, tk=512, lam_init=lam_init)
    y_p, a_last = _ffn(xp, og, od, *ffn_w, tm=256, alpha=alpha, seq_len=tp)
    conv_p = a_last.reshape(nbp, -1, SUBLANES, d_ff)[:, -1, SUBLANES - (CONV_W - 1):, :]

    xs = x_sample.reshape(nbs * ts, d)
    gq, gk, gv, gg, la, dq, dk_s, dv_s, kb, vb = proj(xs)
    og, state_s = _gla(gq, gk, gv, gg, la, gn, state_gla[0], nb=nbs, t=ts, c=ts)
    od = _attn_sample(dq, kb, vb, cache_diff_k[0].reshape(nbs, past_len, dqk),
                      cache_diff_v[0].reshape(nbs, past_len, dqk), lams, dg, nb=nbs, t=ts, lam_init=lam_init)
    y_s, a_s = _ffn(xs, og, od, *ffn_w, tm=nbs * ts, alpha=alpha, seq_len=ts,
                    past=cache_ffn_conv[0].reshape(nbs * (CONV_W - 1), d_ff))
    conv_s = a_s.reshape(nbs, ts, d_ff)[:, ts - (CONV_W - 1):, :]

    heads = dqk // (2 * DIFF_HEAD_DIM)
    k_shape = lambda nb, t: (1, nb, t, heads, 2, DIFF_HEAD_DIM)
    v_shape = lambda nb, t: (1, nb, t, heads, 2 * DIFF_HEAD_DIM)
    return (y_p.reshape(nbp, tp, d), y_s.reshape(nbs, ts, d),
            dk_p.reshape(k_shape(nbp, tp)), dv_p.reshape(v_shape(nbp, tp)), state_p[None], conv_p[None],
            dk_s.reshape(k_shape(nbs, ts)), dv_s.reshape(v_shape(nbs, ts)), state_s[None], conv_s[None])
```

```python
import functools
import math

import numpy as np
import jax
import jax.numpy as jnp
from jax import lax
from jax.experimental import pallas as pl
from jax.experimental.pallas import tpu as pltpu

F32 = jnp.float32
BF16 = jnp.bfloat16

CHUNK = 64
GLA_HEADS = 4
GLA_RANK = 16
GLA_TAU = 16.0
GLA_BLOCK = 16
DIFF_HEAD_DIM = 64
CONV_W = 3
EPS = 1e-5

LANES = 128
SUBLANES = 8
VMEM_LIMIT = 56 * 1024 * 1024

NT_DIMS = (((1,), (1,)), ((), ()))
TN_DIMS = (((0,), (0,)), ((), ()))


def _sigmoid(x):
    return 1.0 / (1.0 + jnp.exp(-x))


def _layer_norm(x, g, b):
    mu = jnp.mean(x, -1, keepdims=True)
    xc = x - mu
    var = jnp.mean(xc * xc, -1, keepdims=True)
    return xc * lax.rsqrt(var + EPS) * g + b


def _resident(shape):
    nd = len(shape)
    return pl.BlockSpec(shape, lambda *_: (0,) * nd, pipeline_mode=pl.Buffered(1))


def _params(sem):
    return pltpu.CompilerParams(dimension_semantics=sem, vmem_limit_bytes=VMEM_LIMIT)


def _proj_kernel(x_ref, wgla_ref, wlow_ref, wa2_ref, ba_ref, wdiff_ref,
                 gq_ref, gk_ref, gv_ref, gg_ref, la_ref, dq_ref, dk_ref, dv_ref, kb_ref, vb_ref,
                 *, qk, vd, dqk):
    xb = x_ref[...].astype(BF16)

    def mm(w_ref, lo, hi):
        return jnp.dot(xb, w_ref[:, lo:hi], preferred_element_type=F32)

    gq_ref[...] = mm(wgla_ref, 0, qk)
    gk_ref[...] = mm(wgla_ref, qk, 2 * qk)
    gv_ref[...] = mm(wgla_ref, 2 * qk, 2 * qk + vd)
    gg_ref[...] = mm(wgla_ref, 2 * qk + vd, 2 * qk + 2 * vd)

    g_low = jnp.dot(xb, wlow_ref[...], preferred_element_type=F32)
    z = jnp.dot(g_low.astype(BF16), wa2_ref[...], preferred_element_type=F32) + ba_ref[...]
    softplus_neg = jnp.maximum(-z, 0.0) + jnp.log1p(jnp.exp(-jnp.abs(z)))
    la_ref[...] = -softplus_neg * (1.0 / GLA_TAU)

    dq_ref[...] = (mm(wdiff_ref, 0, dqk) * (DIFF_HEAD_DIM ** -0.5)).astype(BF16)
    dk = mm(wdiff_ref, dqk, 2 * dqk)
    dk_ref[...] = dk
    kb_ref[...] = dk.astype(BF16)
    dv = mm(wdiff_ref, 2 * dqk, 3 * dqk)
    dv_ref[...] = dv
    vb_ref[...] = dv.astype(BF16)


def _proj(x, w_gla, w_low, w_a2, b_a, w_diff, *, qk, vd, dqk, tm):
    rows, d = x.shape
    tm = min(tm, rows)
    row_spec = lambda n: pl.BlockSpec((tm, n), lambda i: (i, 0))
    out = lambda n, dt: jax.ShapeDtypeStruct((rows, n), dt)
    return pl.pallas_call(
        functools.partial(_proj_kernel, qk=qk, vd=vd, dqk=dqk),
        grid=(rows // tm,),
        in_specs=[row_spec(d), _resident(w_gla.shape), _resident(w_low.shape), _resident(w_a2.shape),
                  _resident(b_a.shape), _resident(w_diff.shape)],
        out_specs=[row_spec(qk), row_spec(qk), row_spec(vd), row_spec(vd), row_spec(qk),
                   row_spec(dqk), row_spec(dqk), row_spec(dqk), row_spec(dqk), row_spec(dqk)],
        out_shape=[out(qk, F32), out(qk, F32), out(vd, F32), out(vd, F32), out(qk, F32),
                   out(dqk, BF16), out(dqk, F32), out(dqk, F32), out(dqk, BF16), out(dqk, BF16)],
        compiler_params=_params(("parallel",)),
        name="proj",
    )(x, w_gla, w_low, w_a2, b_a, w_diff)


def _gla_levels(c):
    d = min(c, 2 * GLA_BLOCK)
    off = []
    b = c // 2
    while b >= d:
        off.append(b)
        b //= 2
    return off, d


def _row_broadcast_refs(cum, group, ref_offset):
    c, n = cum.shape
    parts = [jnp.broadcast_to(cum[p * group + ref_offset:p * group + ref_offset + 1, :], (group, n))
             for p in range(c // group)]
    return parts[0] if len(parts) == 1 else jnp.concatenate(parts, axis=0)


def _gla_kernel(*refs, c, dk, dv, has_s0):
    if has_s0:
        q_ref, k_ref, v_ref, gg_ref, la_ref, gn_ref, s0_ref, o_ref, sout_ref, s_scr = refs
    else:
        q_ref, k_ref, v_ref, gg_ref, la_ref, gn_ref, o_ref, sout_ref, s_scr = refs
    step = pl.program_id(1)

    @pl.when(step == 0)
    def _():
        s_scr[...] = s0_ref[0] if has_s0 else jnp.zeros_like(s_scr)

    row = lax.broadcasted_iota(jnp.int32, (c, c), 0)
    col = lax.broadcasted_iota(jnp.int32, (c, c), 1)
    tri = (col <= row).astype(F32)
    cum_all = jnp.dot(tri, la_ref[...], precision=lax.Precision.HIGHEST, preferred_element_type=F32)

    off_levels, d = _gla_levels(c)
    scale = dk ** -0.5
    gn = gn_ref[...]

    for h in range(GLA_HEADS):
        ks = slice(h * dk, (h + 1) * dk)
        vs = slice(h * dv, (h + 1) * dv)
        cum = cum_all[:, ks]
        qs = q_ref[:, ks] * scale
        kk = k_ref[:, ks]
        vv = v_ref[:, vs].astype(BF16)

        ref = _row_broadcast_refs(cum, d, d // 2 - 1)
        qd = (qs * jnp.exp(cum - ref)).astype(BF16)
        kd = (kk * jnp.exp(ref - cum)).astype(BF16)
        att = lax.dot_general(qd, kd, NT_DIMS, preferred_element_type=F32)
        att = jnp.where((row // d == col // d) & (col <= row), att, 0.0)
        for b in off_levels:
            ref = _row_broadcast_refs(cum, 2 * b, b - 1)
            ql = (qs * jnp.exp(jnp.minimum(cum - ref, 0.0))).astype(BF16)
            kl = (kk * jnp.exp(jnp.minimum(ref - cum, 0.0))).astype(BF16)
            a_l = lax.dot_general(ql, kl, NT_DIMS, preferred_element_type=F32)
            att = jnp.where(((row // b) % 2 == 1) & (col // b == row // b - 1), a_l, att)

        s_h = s_scr[h]
        q_dec = (qs * jnp.exp(cum)).astype(BF16)
        o = (jnp.dot(att.astype(BF16), vv, preferred_element_type=F32)
             + jnp.dot(q_dec, s_h.astype(BF16), preferred_element_type=F32))

        last = cum[c - 1:c, :]
        k_end = (kk * jnp.exp(last - cum)).astype(BF16)
        kv = lax.dot_general(k_end, vv, TN_DIMS, preferred_element_type=F32)
        decay_col = jnp.transpose(jnp.broadcast_to(jnp.exp(last), (dk, dk)))
        decay = jnp.concatenate([decay_col] * (dv // dk), axis=1)
        s_scr[h] = decay * s_h + kv

        ms = jnp.mean(o * o, -1, keepdims=True)
        gate = gg_ref[:, vs]
        o_ref[:, vs] = o * lax.rsqrt(ms + EPS) * gn * (gate * _sigmoid(gate))

    @pl.when(step == pl.num_programs(1) - 1)
    def _():
        sout_ref[0] = s_scr[...]


def _gla(gq, gk, gv, gg, la, gn, s0, *, nb, t, c):
    rows, qk = gq.shape
    vd = gv.shape[1]
    dk, dv = qk // GLA_HEADS, vd // GLA_HEADS
    c = min(c, t)
    nc = t // c
    blk = lambda n: pl.BlockSpec((c, n), lambda b, s: (b * nc + s, 0))
    state_spec = pl.BlockSpec((1, GLA_HEADS, dk, dv), lambda b, s: (b, 0, 0, 0))
    in_specs = [blk(qk), blk(qk), blk(vd), blk(vd), blk(qk), _resident(gn.shape)]
    args = [gq, gk, gv, gg, la, gn]
    if s0 is not None:
        in_specs.append(state_spec)
        args.append(s0)
    return pl.pallas_call(
        functools.partial(_gla_kernel, c=c, dk=dk, dv=dv, has_s0=s0 is not None),
        grid=(nb, nc),
        in_specs=in_specs,
        out_specs=[blk(vd), state_spec],
        out_shape=[jax.ShapeDtypeStruct((rows, vd), F32),
                   jax.ShapeDtypeStruct((nb, GLA_HEADS, dk, dv), F32)],
        scratch_shapes=[pltpu.VMEM((GLA_HEADS, dk, dv), F32)],
        compiler_params=_params(("parallel", "arbitrary")),
        name="gla",
    )(*args)


def _stack_half_queries(q):
    lane = lax.broadcasted_iota(jnp.int32, q.shape, 1)
    zero = jnp.zeros_like(q)
    return jnp.concatenate([jnp.where(lane < DIFF_HEAD_DIM, q, zero),
                            jnp.where(lane >= DIFF_HEAD_DIM, q, zero)], axis=0)


def _lambda(lq1_ref, lk1_ref, lq2_ref, lk2_ref, lam_init):
    return (jnp.exp(jnp.sum(lq1_ref[...] * lk1_ref[...], -1, keepdims=True))
            - jnp.exp(jnp.sum(lq2_ref[...] * lk2_ref[...], -1, keepdims=True)) + lam_init)


def _diff_finish(acc, l, lam, g, lam_init, tq):
    o = acc * (1.0 / l)
    od = o[:tq] - lam * o[tq:]
    ms = jnp.mean(od * od, -1, keepdims=True)
    return od * lax.rsqrt(ms + EPS) * g * (1.0 - lam_init)


def _attn_prompt_kernel(q_ref, k_ref, v_ref, lq1_ref, lk1_ref, lq2_ref, lk2_ref, g_ref, o_ref,
                        qs_scr, s_scr, p_scr, m_scr, l_scr, a_scr, acc_scr, *, tq, tk, slab, lam_init):
    i = pl.program_id(2)
    q_first = i * tq
    qs_scr[...] = _stack_half_queries(q_ref[...])
    m_scr[...] = jnp.full(m_scr.shape, -jnp.inf, F32)
    l_scr[...] = jnp.zeros(l_scr.shape, F32)
    acc_scr[...] = jnp.zeros(acc_scr.shape, F32)
    reps = tk // LANES

    def tile(j, carry, masked):
        start = pl.multiple_of(j * tk, tk)
        s_scr[...] = lax.dot_general(qs_scr[...], k_ref[pl.ds(start, tk), :], NT_DIMS,
                                     preferred_element_type=F32)
        for r0 in range(0, 2 * tq, slab):
            rows = slice(r0, r0 + slab)
            s = s_scr[rows, :]
            if masked:
                q_pos = q_first + (r0 % tq) + lax.broadcasted_iota(jnp.int32, (slab, 1), 0)
                limit = (q_pos // CHUNK + 1) * CHUNK - start
                s = jnp.where(lax.broadcasted_iota(jnp.int32, s.shape, 1) < limit, s, -jnp.inf)
            m_prev = m_scr[rows, :]
            m_next = jnp.maximum(m_prev, jnp.max(s, -1, keepdims=True))
            alpha = jnp.exp(m_prev - m_next)
            p = jnp.exp(s - pltpu.repeat(m_next, reps, 1))
            p_lanes = p[:, 0:LANES]
            for c in range(1, reps):
                p_lanes = p_lanes + p[:, c * LANES:(c + 1) * LANES]
            l_scr[rows, :] = alpha * l_scr[rows, :] + p_lanes
            m_scr[rows, :] = m_next
            a_scr[rows, :] = alpha
            p_scr[rows, :] = p.astype(BF16)
        acc_scr[...] = a_scr[...] * acc_scr[...] + jnp.dot(p_scr[...], v_ref[pl.ds(start, tk), :],
                                                           preferred_element_type=F32)
        return carry

    n_full = (q_first + CHUNK) // tk
    n_any = (q_first + tq + tk - 1) // tk
    def tile_pair(jj, carry):
        tile(2 * jj, carry, masked=False)
        return tile(2 * jj + 1, carry, masked=False)

    lax.fori_loop(0, n_full // 2, tile_pair, 0)
    lax.fori_loop(2 * (n_full // 2), n_full, functools.partial(tile, masked=False), 0)
    lax.fori_loop(n_full, n_any, functools.partial(tile, masked=True), 0)

    lam = _lambda(lq1_ref, lk1_ref, lq2_ref, lk2_ref, lam_init)
    l = jnp.sum(l_scr[...], -1, keepdims=True)
    o_ref[...] = _diff_finish(acc_scr[...], l, lam, g_ref[...], lam_init, tq)


def _attn_prompt(dq, kb, vb, lams, g, *, nb, t, tq, tk, lam_init, slab=64):
    rows, dqk = dq.shape
    hw = 2 * DIFF_HEAD_DIM
    heads = dqk // hw
    tq, tk = min(tq, t), min(tk, t)
    nq = t // tq
    kv_spec = pl.BlockSpec((t, hw), lambda b, h, i: (b, h))
    q_spec = pl.BlockSpec((tq, hw), lambda b, h, i: (b * nq + i, h))
    small = [_resident(a.shape) for a in (*lams, g)]
    stat = pltpu.VMEM((2 * tq, LANES), F32)
    return pl.pallas_call(
        functools.partial(_attn_prompt_kernel, tq=tq, tk=tk, slab=slab, lam_init=lam_init),
        grid=(nb, heads, nq),
        in_specs=[q_spec, kv_spec, kv_spec, *small],
        out_specs=q_spec,
        out_shape=jax.ShapeDtypeStruct((rows, dqk), F32),
        scratch_shapes=[pltpu.VMEM((2 * tq, hw), BF16), pltpu.VMEM((2 * tq, tk), F32),
                        pltpu.VMEM((2 * tq, tk), BF16), stat, stat, stat, pltpu.VMEM((2 * tq, hw), F32)],
        compiler_params=_params(("parallel", "parallel", "arbitrary")),
        name="attn_prompt",
    )(dq, kb, vb, *lams, g)


def _attn_sample_kernel(q_ref, kn_ref, vn_ref, kc_ref, vc_ref, lq1_ref, lk1_ref, lq2_ref, lk2_ref, g_ref,
                        o_ref, *, heads, lam_init):
    tq = q_ref.shape[0]
    hw = 2 * DIFF_HEAD_DIM
    lam = _lambda(lq1_ref, lk1_ref, lq2_ref, lk2_ref, lam_init)
    g = g_ref[...]
    for h in range(heads):
        hs = slice(h * hw, (h + 1) * hw)
        qs = _stack_half_queries(q_ref[:, hs])
        k_past = kc_ref[0, :, hs].astype(BF16)
        v_past = vc_ref[0, :, hs].astype(BF16)
        s_past = lax.dot_general(qs, k_past, NT_DIMS, preferred_element_type=F32)
        s_new = lax.dot_general(qs, kn_ref[:, hs], NT_DIMS, preferred_element_type=F32)
        m = jnp.maximum(jnp.max(s_past, -1, keepdims=True), jnp.max(s_new, -1, keepdims=True))
        p_past = jnp.exp(s_past - m)
        p_new = jnp.exp(s_new - m)
        l = jnp.sum(p_past, -1, keepdims=True) + jnp.sum(p_new, -1, keepdims=True)
        acc = (jnp.dot(p_past.astype(BF16), v_past, preferred_element_type=F32)
               + jnp.dot(p_new.astype(BF16), vn_ref[:, hs], preferred_element_type=F32))
        o_ref[:, hs] = _diff_finish(acc, l, lam, g, lam_init, tq)


def _attn_sample(dq, kb, vb, k_cache, v_cache, lams, g, *, nb, t, lam_init):
    rows, dqk = dq.shape
    past = k_cache.shape[1]
    q_pos = past + np.arange(t)
    assert ((q_pos // CHUNK + 1) * CHUNK >= past + t).all(), "sample queries must see every cached and new key"
    new_spec = pl.BlockSpec((t, dqk), lambda b: (b, 0))
    cache_spec = pl.BlockSpec((1, past, dqk), lambda b: (b, 0, 0))
    small = [_resident(a.shape) for a in (*lams, g)]
    return pl.pallas_call(
        functools.partial(_attn_sample_kernel, heads=dqk // (2 * DIFF_HEAD_DIM), lam_init=lam_init),
        grid=(nb,),
        in_specs=[new_spec, new_spec, new_spec, cache_spec, cache_spec, *small],
        out_specs=new_spec,
        out_shape=jax.ShapeDtypeStruct((rows, dqk), F32),
        compiler_params=_params(("parallel",)),
        name="attn_sample",
    )(dq, kb, vb, k_cache, v_cache, *lams, g)


def _gelu_tanh(x):
    return x * (0.5 * (1.0 + jnp.tanh(math.sqrt(2.0 / math.pi) * (x + 0.044715 * (x * x * x)))))


def _ffn_kernel(*refs, tm, d_ff, fc, alpha, blocks_per_seq, seg):
    sample = seg is not None
    if sample:
        (x_ref, og_ref, od_ref, wg_ref, wo_ref, l1g_ref, l1b_ref, wup_ref, cw_ref, cb_ref, wdn_ref,
         l2g_ref, l2b_ref, past_ref, sel1_ref, sel2_ref, y_ref, a_ref, a_scr) = refs
    else:
        (x_ref, og_ref, od_ref, wg_ref, wo_ref, l1g_ref, l1b_ref, wup_ref, cw_ref, cb_ref, wdn_ref,
         l2g_ref, l2b_ref, y_ref, a_ref, a_scr, carry_scr) = refs
    d = x_ref.shape[1]
    x = x_ref[...]
    xb = x.astype(BF16)
    gate_a = jnp.dot(xb, wg_ref[:, 0:d], preferred_element_type=F32)
    gate_b = jnp.dot(xb, wg_ref[:, d:2 * d], preferred_element_type=F32)
    mixed = _sigmoid(gate_a) * og_ref[...] + _sigmoid(gate_b) * od_ref[...]
    x1 = _layer_norm(alpha * x + jnp.dot(mixed.astype(BF16), wo_ref[...], preferred_element_type=F32),
                     l1g_ref[...], l1b_ref[...])
    x1b = x1.astype(BF16)

    pad = SUBLANES
    if sample:
        r = lax.broadcasted_iota(jnp.int32, (tm, fc), 0) % seg
    else:
        first = pl.program_id(0) % blocks_per_seq == 0

    acc = jnp.zeros((tm, d), F32)
    for c in range(d_ff // fc):
        cs = slice(c * fc, (c + 1) * fc)
        a = jnp.dot(x1b, wup_ref[:, cs], preferred_element_type=F32)
        gate = jnp.dot(x1b, wup_ref[:, d_ff + c * fc:d_ff + (c + 1) * fc], preferred_element_type=F32)
        a_scr[pad:pad + tm, :] = a
        if sample:
            a_scr[0:pad, :] = jnp.zeros((pad, fc), F32)
            past = past_ref[:, cs]
            hist1 = jnp.dot(sel1_ref[...], past, precision=lax.Precision.HIGHEST, preferred_element_type=F32)
            hist2 = jnp.dot(sel2_ref[...], past, precision=lax.Precision.HIGHEST, preferred_element_type=F32)
            a1 = jnp.where(r == 0, 0.0, a_scr[pad - 1:pad - 1 + tm, :]) + hist1
            a2 = jnp.where(r < 2, 0.0, a_scr[pad - 2:pad - 2 + tm, :]) + hist2
            a_ref[:, cs] = a
        else:
            a_scr[0:pad, :] = jnp.where(first, 0.0, carry_scr[:, cs])
            a1 = a_scr[pad - 1:pad - 1 + tm, :]
            a2 = a_scr[pad - 2:pad - 2 + tm, :]
            carry_scr[:, cs] = a[tm - pad:tm, :]
            a_ref[0, :, cs] = a[tm - pad:tm, :]
        conv = cb_ref[:, cs] + a2 * cw_ref[0:1, cs] + a1 * cw_ref[1:2, cs] + a * cw_ref[2:3, cs]
        hidden = _gelu_tanh(conv) * gate
        acc = acc + jnp.dot(hidden.astype(BF16), wdn_ref[cs, :], preferred_element_type=F32)

    y_ref[...] = _layer_norm(alpha * x1 + acc, l2g_ref[...], l2b_ref[...])


def _ffn(x, og, od, w_gate, w_out, l1g, l1b, w_up, conv_w, conv_b, w_down, l2g, l2b,
         *, tm, alpha, seq_len, past=None, fc=256):
    rows, d = x.shape
    d_ff = w_down.shape[0]
    tm = min(tm, rows)
    nblk = rows // tm
    sample = past is not None
    row_spec = pl.BlockSpec((tm, d), lambda i: (i, 0))
    weights = [w_gate, w_out, l1g, l1b, w_up, conv_w, conv_b, w_down, l2g, l2b]
    args = [x, og, od, *weights]
    scratch = [pltpu.VMEM((tm + SUBLANES, fc), F32)]
    if sample:
        nseq = tm // seq_len
        sel1 = np.zeros((tm, past.shape[0]), np.float32)
        sel2 = np.zeros((tm, past.shape[0]), np.float32)
        for s in range(nseq):
            sel1[s * seq_len, 2 * s + 1] = 1.0
            sel2[s * seq_len, 2 * s] = 1.0
            sel2[s * seq_len + 1, 2 * s + 1] = 1.0
        assert nblk == 1
        args += [past, jnp.asarray(sel1), jnp.asarray(sel2)]
        a_shape = jax.ShapeDtypeStruct((rows, d_ff), F32)
        a_spec = pl.BlockSpec((tm, d_ff), lambda i: (i, 0))
        kern = functools.partial(_ffn_kernel, tm=tm, d_ff=d_ff, fc=fc, alpha=alpha, blocks_per_seq=None, seg=seq_len)
    else:
        a_shape = jax.ShapeDtypeStruct((nblk, SUBLANES, d_ff), F32)
        a_spec = pl.BlockSpec((1, SUBLANES, d_ff), lambda i: (i, 0, 0))
        scratch.append(pltpu.VMEM((SUBLANES, d_ff), F32))
        kern = functools.partial(_ffn_kernel, tm=tm, d_ff=d_ff, fc=fc, alpha=alpha,
                                 blocks_per_seq=seq_len // tm, seg=None)
    in_specs = [row_spec, row_spec, row_spec] + [_resident(a.shape) for a in args[3:]]
    return pl.pallas_call(
        kern,
        grid=(nblk,),
        in_specs=in_specs,
        out_specs=[row_spec, a_spec],
        out_shape=[jax.ShapeDtypeStruct((rows, d), F32), a_shape],
        scratch_shapes=scratch,
        compiler_params=_params(("arbitrary",)),
        name="ffn",
    )(*args)


def kernel(x_prompt, x_sample, cache_diff_k, cache_diff_v, state_gla, cache_ffn_conv, w_in, w_a2, b_a, gla_norm_g, lam_q1, lam_k1, lam_q2, lam_k2, diff_norm_g, w_out, ln1_g, ln1_b, w_up, conv_w, conv_b, w_down, ln2_g, ln2_b):
    depth = w_in.shape[0]
    assert depth == 1, "single-layer step"
    nbp, tp, d = x_prompt.shape
    nbs, ts, _ = x_sample.shape
    past_len = cache_diff_k.shape[2]
    qk = w_a2.shape[2]
    vd = GLA_HEADS * gla_norm_g.shape[1]
    dqk = cache_diff_v.shape[3] * cache_diff_v.shape[4]
    d_ff = w_down.shape[1]
    alpha = (2.0 * depth) ** 0.25
    lam_init = 0.8 - 0.6 * math.exp(-0.3 * 0)

    w = w_in[0]
    o_low = 2 * qk + 2 * vd
    o_diff = o_low + GLA_RANK
    o_gate = o_diff + 3 * dqk
    w_gla = w[:, :o_low].astype(BF16)
    w_low = jnp.pad(w[:, o_low:o_diff], ((0, 0), (0, LANES - GLA_RANK))).astype(BF16)
    w_diff = w[:, o_diff:o_gate].astype(BF16)
    w_gate = w[:, o_gate:].astype(BF16)
    w_a2p = jnp.pad(w_a2[0], ((0, LANES - GLA_RANK), (0, 0))).astype(BF16)
    row2 = lambda a: a.reshape(1, -1)
    lams = [row2(lam_q1[0]), row2(lam_k1[0]), row2(lam_q2[0]), row2(lam_k2[0])]
    ffn_w = (w_gate, w_out[0].astype(BF16), row2(ln1_g[0]), row2(ln1_b[0]), w_up[0].astype(BF16), conv_w[0],
             row2(conv_b[0]), w_down[0].astype(BF16), row2(ln2_g[0]), row2(ln2_b[0]))
    gn = row2(gla_norm_g[0])
    dg = row2(diff_norm_g[0])
    proj = functools.partial(_proj, w_gla=w_gla, w_low=w_low, w_a2=w_a2p, b_a=row2(b_a[0]), w_diff=w_diff,
                             qk=qk, vd=vd, dqk=dqk, tm=256)

    xp = x_prompt.reshape(nbp * tp, d)
    gq, gk, gv, gg, la, dq, dk_p, dv_p, kb, vb = proj(xp)
    og, state_p = _gla(gq, gk, gv, gg, la, gn, None, nb=nbp, t=tp, c=128)
    od = _attn_prompt(dq, kb, vb, lams, dg, nb=nbp, t=tp, tq=512, tk=512, lam_init=lam_init)
    y_p, a_last = _ffn(xp, og, od, *ffn_w, tm=256, alpha=alpha, seq_len=tp)
    conv_p = a_last.reshape(nbp, -1, SUBLANES, d_ff)[:, -1, SUBLANES - (CONV_W - 1):, :]

    xs = x_sample.reshape(nbs * ts, d)
    gq, gk, gv, gg, la, dq, dk_s, dv_s, kb, vb = proj(xs)
    og, state_s = _gla(gq, gk, gv, gg, la, gn, state_gla[0], nb=nbs, t=ts, c=ts)
    od = _attn_sample(dq, kb, vb, cache_diff_k[0].reshape(nbs, past_len, dqk),
                      cache_diff_v[0].reshape(nbs, past_len, dqk), lams, dg, nb=nbs, t=ts, lam_init=lam_init)
    y_s, a_s = _ffn(xs, og, od, *ffn_w, tm=nbs * ts, alpha=alpha, seq_len=ts,
                    past=cache_ffn_conv[0].reshape(nbs * (CONV_W - 1), d_ff))
    conv_s = a_s.reshape(nbs, ts, d_ff)[:, ts - (CONV_W - 1):, :]

    heads = dqk // (2 * DIFF_HEAD_DIM)
    k_shape = lambda nb, t: (1, nb, t, heads, 2, DIFF_HEAD_DIM)
    v_shape = lambda nb, t: (1, nb, t, heads, 2 * DIFF_HEAD_DIM)
    return (y_p.reshape(nbp, tp, d), y_s.reshape(nbs, ts, d),
            dk_p.reshape(k_shape(nbp, tp)), dv_p.reshape(v_shape(nbp, tp)), state_p[None], conv_p[None],
            dk_s.reshape(k_shape(nbs, ts)), dv_s.reshape(v_shape(nbs, ts)), state_s[None], conv_s[None])
```

```python
import functools
import math

import numpy as np
import jax
import jax.numpy as jnp
from jax import lax
from jax.experimental import pallas as pl
from jax.experimental.pallas import tpu as pltpu

F32 = jnp.float32
BF16 = jnp.bfloat16

CHUNK = 64
GLA_HEADS = 4
GLA_RANK = 16
GLA_TAU = 16.0
GLA_BLOCK = 16
DIFF_HEAD_DIM = 64
CONV_W = 3
EPS = 1e-5
LOG2_E = math.log2(math.e)

LANES = 128
SUBLANES = 8
VMEM_LIMIT = 56 * 1024 * 1024

NT_DIMS = (((1,), (1,)), ((), ()))
TN_DIMS = (((0,), (0,)), ((), ()))


def _sigmoid(x):
    return 1.0 / (1.0 + jnp.exp(-x))


def _layer_norm(x, g, b):
    mu = jnp.mean(x, -1, keepdims=True)
    xc = x - mu
    var = jnp.mean(xc * xc, -1, keepdims=True)
    return xc * lax.rsqrt(var + EPS) * g + b


def _resident(shape):
    nd = len(shape)
    return pl.BlockSpec(shape, lambda *_: (0,) * nd, pipeline_mode=pl.Buffered(1))


def _params(sem):
    return pltpu.CompilerParams(dimension_semantics=sem, vmem_limit_bytes=VMEM_LIMIT)


def _proj_kernel(x_ref, wgla_ref, wlow_ref, wa2_ref, ba_ref, wdiff_ref,
                 gq_ref, gk_ref, gv_ref, gg_ref, la_ref, dq_ref, dk_ref, dv_ref, kb_ref, vb_ref,
                 *, qk, vd, dqk):
    xb = x_ref[...].astype(BF16)

    def mm(w_ref, lo, hi):
        return jnp.dot(xb, w_ref[:, lo:hi], preferred_element_type=F32)

    gq_ref[...] = mm(wgla_ref, 0, qk)
    gk_ref[...] = mm(wgla_ref, qk, 2 * qk)
    gv_ref[...] = mm(wgla_ref, 2 * qk, 2 * qk + vd)
    gg_ref[...] = mm(wgla_ref, 2 * qk + vd, 2 * qk + 2 * vd)

    g_low = jnp.dot(xb, wlow_ref[...], preferred_element_type=F32)
    z = jnp.dot(g_low.astype(BF16), wa2_ref[...], preferred_element_type=F32) + ba_ref[...]
    softplus_neg = jnp.maximum(-z, 0.0) + jnp.log1p(jnp.exp(-jnp.abs(z)))
    la_ref[...] = -softplus_neg * (1.0 / GLA_TAU)

    dq_ref[...] = (mm(wdiff_ref, 0, dqk) * (DIFF_HEAD_DIM ** -0.5 * LOG2_E)).astype(BF16)
    dk = mm(wdiff_ref, dqk, 2 * dqk)
    dk_ref[...] = dk
    kb_ref[...] = dk.astype(BF16)
    dv = mm(wdiff_ref, 2 * dqk, 3 * dqk)
    dv_ref[...] = dv
    vb_ref[...] = dv.astype(BF16)


def _proj(x, w_gla, w_low, w_a2, b_a, w_diff, *, qk, vd, dqk, tm):
    rows, d = x.shape
    tm = min(tm, rows)
    row_spec = lambda n: pl.BlockSpec((tm, n), lambda i: (i, 0))
    out = lambda n, dt: jax.ShapeDtypeStruct((rows, n), dt)
    return pl.pallas_call(
        functools.partial(_proj_kernel, qk=qk, vd=vd, dqk=dqk),
        grid=(rows // tm,),
        in_specs=[row_spec(d), _resident(w_gla.shape), _resident(w_low.shape), _resident(w_a2.shape),
                  _resident(b_a.shape), _resident(w_diff.shape)],
        out_specs=[row_spec(qk), row_spec(qk), row_spec(vd), row_spec(vd), row_spec(qk),
                   row_spec(dqk), row_spec(dqk), row_spec(dqk), row_spec(dqk), row_spec(dqk)],
        out_shape=[out(qk, F32), out(qk, F32), out(vd, F32), out(vd, F32), out(qk, F32),
                   out(dqk, BF16), out(dqk, F32), out(dqk, F32), out(dqk, BF16), out(dqk, BF16)],
        compiler_params=_params(("parallel",)),
        name="proj",
    )(x, w_gla, w_low, w_a2, b_a, w_diff)


def _gla_levels(c):
    d = min(c, 2 * GLA_BLOCK)
    off = []
    b = c // 2
    while b >= d:
        off.append(b)
        b //= 2
    return off, d


def _row_broadcast_refs(cum, group, ref_offset):
    c, n = cum.shape
    parts = [jnp.broadcast_to(cum[p * group + ref_offset:p * group + ref_offset + 1, :], (group, n))
             for p in range(c // group)]
    return parts[0] if len(parts) == 1 else jnp.concatenate(parts, axis=0)


def _gla_kernel(*refs, c, nsub, dk, dv, has_s0):
    if has_s0:
        q_ref, k_ref, v_ref, gg_ref, la_ref, gn_ref, s0_ref, o_ref, sout_ref, s_scr = refs
    else:
        q_ref, k_ref, v_ref, gg_ref, la_ref, gn_ref, o_ref, sout_ref, s_scr = refs
    step = pl.program_id(1)

    @pl.when(step == 0)
    def _():
        s_scr[...] = s0_ref[0] if has_s0 else jnp.zeros_like(s_scr)

    for sub in range(nsub):
        rows = slice(sub * c, (sub + 1) * c)
        _gla_chunk(q_ref, k_ref, v_ref, gg_ref, la_ref, gn_ref, o_ref, s_scr, rows, c=c, dk=dk, dv=dv)

    @pl.when(step == pl.num_programs(1) - 1)
    def _():
        sout_ref[0] = s_scr[...]


def _gla_chunk(q_ref, k_ref, v_ref, gg_ref, la_ref, gn_ref, o_ref, s_scr, rows, *, c, dk, dv):
    row = lax.broadcasted_iota(jnp.int32, (c, c), 0)
    col = lax.broadcasted_iota(jnp.int32, (c, c), 1)
    tri = (col <= row).astype(F32)
    cum_all = jnp.dot(tri, la_ref[rows, :], precision=lax.Precision.HIGHEST, preferred_element_type=F32)

    off_levels, d = _gla_levels(c)
    scale = dk ** -0.5
    gn = gn_ref[...]

    for h in range(GLA_HEADS):
        ks = slice(h * dk, (h + 1) * dk)
        vs = slice(h * dv, (h + 1) * dv)
        cum = cum_all[:, ks]
        qs = q_ref[rows, ks] * scale
        kk = k_ref[rows, ks]
        vv = v_ref[rows, vs].astype(BF16)

        ref = _row_broadcast_refs(cum, d, d // 2 - 1)
        qd = (qs * jnp.exp(cum - ref)).astype(BF16)
        kd = (kk * jnp.exp(ref - cum)).astype(BF16)
        att = lax.dot_general(qd, kd, NT_DIMS, preferred_element_type=F32)
        att = jnp.where((row // d == col // d) & (col <= row), att, 0.0)
        for b in off_levels:
            ref = _row_broadcast_refs(cum, 2 * b, b - 1)
            ql = (qs * jnp.exp(jnp.minimum(cum - ref, 0.0))).astype(BF16)
            kl = (kk * jnp.exp(jnp.minimum(ref - cum, 0.0))).astype(BF16)
            a_l = lax.dot_general(ql, kl, NT_DIMS, preferred_element_type=F32)
            att = jnp.where(((row // b) % 2 == 1) & (col // b == row // b - 1), a_l, att)

        s_h = s_scr[h]
        q_dec = (qs * jnp.exp(cum)).astype(BF16)
        o = (jnp.dot(att.astype(BF16), vv, preferred_element_type=F32)
             + jnp.dot(q_dec, s_h.astype(BF16), preferred_element_type=F32))

        last = cum[c - 1:c, :]
        k_end = (kk * jnp.exp(last - cum)).astype(BF16)
        kv = lax.dot_general(k_end, vv, TN_DIMS, preferred_element_type=F32)
        decay_col = jnp.transpose(jnp.broadcast_to(jnp.exp(last), (dk, dk)))
        decay = jnp.concatenate([decay_col] * (dv // dk), axis=1)
        s_scr[h] = decay * s_h + kv

        ms = jnp.mean(o * o, -1, keepdims=True)
        gate = gg_ref[rows, vs]
        o_ref[rows, vs] = o * lax.rsqrt(ms + EPS) * gn * (gate * _sigmoid(gate))


def _gla(gq, gk, gv, gg, la, gn, s0, *, nb, t, c, nsub=1):
    rows, qk = gq.shape
    vd = gv.shape[1]
    dk, dv = qk // GLA_HEADS, vd // GLA_HEADS
    c = min(c, t)
    nsub = min(nsub, t // c)
    nc = t // (c * nsub)
    blk = lambda n: pl.BlockSpec((c * nsub, n), lambda b, s: (b * nc + s, 0))
    state_spec = pl.BlockSpec((1, GLA_HEADS, dk, dv), lambda b, s: (b, 0, 0, 0))
    in_specs = [blk(qk), blk(qk), blk(vd), blk(vd), blk(qk), _resident(gn.shape)]
    args = [gq, gk, gv, gg, la, gn]
    if s0 is not None:
        in_specs.append(state_spec)
        args.append(s0)
    return pl.pallas_call(
        functools.partial(_gla_kernel, c=c, nsub=nsub, dk=dk, dv=dv, has_s0=s0 is not None),
        grid=(nb, nc),
        in_specs=in_specs,
        out_specs=[blk(vd), state_spec],
        out_shape=[jax.ShapeDtypeStruct((rows, vd), F32),
                   jax.ShapeDtypeStruct((nb, GLA_HEADS, dk, dv), F32)],
        scratch_shapes=[pltpu.VMEM((GLA_HEADS, dk, dv), F32)],
        compiler_params=_params(("parallel", "arbitrary")),
        name="gla",
    )(*args)


def _stack_half_queries(q):
    lane = lax.broadcasted_iota(jnp.int32, q.shape, 1)
    zero = jnp.zeros_like(q)
    return jnp.concatenate([jnp.where(lane < DIFF_HEAD_DIM, q, zero),
                            jnp.where(lane >= DIFF_HEAD_DIM, q, zero)], axis=0)


def _lambda(lq1_ref, lk1_ref, lq2_ref, lk2_ref, lam_init):
    return (jnp.exp(jnp.sum(lq1_ref[...] * lk1_ref[...], -1, keepdims=True))
            - jnp.exp(jnp.sum(lq2_ref[...] * lk2_ref[...], -1, keepdims=True)) + lam_init)


def _diff_finish(acc, l, lam, g, lam_init, tq):
    o = acc * (1.0 / l)
    od = o[:tq] - lam * o[tq:]
    ms = jnp.mean(od * od, -1, keepdims=True)
    return od * lax.rsqrt(ms + EPS) * g * (1.0 - lam_init)


def _attn_prompt_kernel(q_ref, k_ref, v_ref, lq1_ref, lk1_ref, lq2_ref, lk2_ref, g_ref, o_ref,
                        qs_scr, s_scr, p_scr, m_scr, l_scr, a_scr, acc_scr, *, tq, tk, slab, lam_init):
    i = pl.program_id(2)
    q_first = i * tq
    qs_scr[...] = _stack_half_queries(q_ref[...])
    m_scr[...] = jnp.full(m_scr.shape, -jnp.inf, F32)
    l_scr[...] = jnp.zeros(l_scr.shape, F32)
    acc_scr[...] = jnp.zeros(acc_scr.shape, F32)
    reps = tk // LANES

    def tile(j, carry, masked):
        start = pl.multiple_of(j * tk, tk)
        s_scr[...] = lax.dot_general(qs_scr[...], k_ref[pl.ds(start, tk), :], NT_DIMS,
                                     preferred_element_type=F32)
        for r0 in range(0, 2 * tq, slab):
            rows = slice(r0, r0 + slab)
            s = s_scr[rows, :]
            if masked:
                q_pos = q_first + (r0 % tq) + lax.broadcasted_iota(jnp.int32, (slab, 1), 0)
                limit = (q_pos // CHUNK + 1) * CHUNK - start
                s = jnp.where(lax.broadcasted_iota(jnp.int32, s.shape, 1) < limit, s, -jnp.inf)
            m_prev = m_scr[rows, :]
            m_next = jnp.maximum(m_prev, jnp.max(s, -1, keepdims=True))
            alpha = jnp.exp2(m_prev - m_next)
            p = jnp.exp2(s - jnp.concatenate([m_next] * reps, axis=1))
            p_lanes = p[:, 0:LANES]
            for c in range(1, reps):
                p_lanes = p_lanes + p[:, c * LANES:(c + 1) * LANES]
            l_scr[rows, :] = alpha * l_scr[rows, :] + p_lanes
            m_scr[rows, :] = m_next
            a_scr[rows, :] = alpha
            p_scr[rows, :] = p.astype(BF16)
        acc_scr[...] = a_scr[...] * acc_scr[...] + jnp.dot(p_scr[...], v_ref[pl.ds(start, tk), :],
                                                           preferred_element_type=F32)
        return carry

    n_full = (q_first + CHUNK) // tk
    n_any = (q_first + tq + tk - 1) // tk
    def tile_pair(jj, carry):
        tile(2 * jj, carry, masked=False)
        return tile(2 * jj + 1, carry, masked=False)

    lax.fori_loop(0, n_full // 2, tile_pair, 0)
    lax.fori_loop(2 * (n_full // 2), n_full, functools.partial(tile, masked=False), 0)
    lax.fori_loop(n_full, n_any, functools.partial(tile, masked=True), 0)

    lam = _lambda(lq1_ref, lk1_ref, lq2_ref, lk2_ref, lam_init)
    l = jnp.sum(l_scr[...], -1, keepdims=True)
    o_ref[...] = _diff_finish(acc_scr[...], l, lam, g_ref[...], lam_init, tq)


def _attn_prompt(dq, kb, vb, lams, g, *, nb, t, tq, tk, lam_init, slab=64):
    rows, dqk = dq.shape
    hw = 2 * DIFF_HEAD_DIM
    heads = dqk // hw
    tq, tk = min(tq, t), min(tk, t)
    nq = t // tq
    kv_spec = pl.BlockSpec((t, hw), lambda b, h, i: (b, h))
    q_spec = pl.BlockSpec((tq, hw), lambda b, h, i: (b * nq + i, h))
    small = [_resident(a.shape) for a in (*lams, g)]
    stat = pltpu.VMEM((2 * tq, LANES), F32)
    return pl.pallas_call(
        functools.partial(_attn_prompt_kernel, tq=tq, tk=tk, slab=slab, lam_init=lam_init),
        grid=(nb, heads, nq),
        in_specs=[q_spec, kv_spec, kv_spec, *small],
        out_specs=q_spec,
        out_shape=jax.ShapeDtypeStruct((rows, dqk), F32),
        scratch_shapes=[pltpu.VMEM((2 * tq, hw), BF16), pltpu.VMEM((2 * tq, tk), F32),
                        pltpu.VMEM((2 * tq, tk), BF16), stat, stat, stat, pltpu.VMEM((2 * tq, hw), F32)],
        compiler_params=_params(("parallel", "parallel", "arbitrary")),
        name="attn_prompt",
    )(dq, kb, vb, *lams, g)


def _attn_sample_kernel(q_ref, kn_ref, vn_ref, kc_ref, vc_ref, lq1_ref, lk1_ref, lq2_ref, lk2_ref, g_ref,
                        o_ref, *, heads, lam_init):
    tq = q_ref.shape[0]
    hw = 2 * DIFF_HEAD_DIM
    lam = _lambda(lq1_ref, lk1_ref, lq2_ref, lk2_ref, lam_init)
    g = g_ref[...]
    for h in range(heads):
        hs = slice(h * hw, (h + 1) * hw)
        qs = _stack_half_queries(q_ref[:, hs])
        k_past = kc_ref[0, :, hs].astype(BF16)
        v_past = vc_ref[0, :, hs].astype(BF16)
        s_past = lax.dot_general(qs, k_past, NT_DIMS, preferred_element_type=F32)
        s_new = lax.dot_general(qs, kn_ref[:, hs], NT_DIMS, preferred_element_type=F32)
        m = jnp.maximum(jnp.max(s_past, -1, keepdims=True), jnp.max(s_new, -1, keepdims=True))
        p_past = jnp.exp2(s_past - m)
        p_new = jnp.exp2(s_new - m)
        l = jnp.sum(p_past, -1, keepdims=True) + jnp.sum(p_new, -1, keepdims=True)
        acc = (jnp.dot(p_past.astype(BF16), v_past, preferred_element_type=F32)
               + jnp.dot(p_new.astype(BF16), vn_ref[:, hs], preferred_element_type=F32))
        o_ref[:, hs] = _diff_finish(acc, l, lam, g, lam_init, tq)


def _attn_sample(dq, kb, vb, k_cache, v_cache, lams, g, *, nb, t, lam_init):
    rows, dqk = dq.shape
    past = k_cache.shape[1]
    q_pos = past + np.arange(t)
    assert ((q_pos // CHUNK + 1) * CHUNK >= past + t).all(), "sample queries must see every cached and new key"
    new_spec = pl.BlockSpec((t, dqk), lambda b: (b, 0))
    cache_spec = pl.BlockSpec((1, past, dqk), lambda b: (b, 0, 0))
    small = [_resident(a.shape) for a in (*lams, g)]
    return pl.pallas_call(
        functools.partial(_attn_sample_kernel, heads=dqk // (2 * DIFF_HEAD_DIM), lam_init=lam_init),
        grid=(nb,),
        in_specs=[new_spec, new_spec, new_spec, cache_spec, cache_spec, *small],
        out_specs=new_spec,
        out_shape=jax.ShapeDtypeStruct((rows, dqk), F32),
        compiler_params=_params(("parallel",)),
        name="attn_sample",
    )(dq, kb, vb, k_cache, v_cache, *lams, g)


def _gelu_tanh(x):
    return x * (0.5 * (1.0 + jnp.tanh(math.sqrt(2.0 / math.pi) * (x + 0.044715 * (x * x * x)))))


def _ffn_kernel(*refs, tm, d_ff, fc, alpha, blocks_per_seq, seg):
    sample = seg is not None
    if sample:
        (x_ref, og_ref, od_ref, wg_ref, wo_ref, l1g_ref, l1b_ref, wup_ref, cw_ref, cb_ref, wdn_ref,
         l2g_ref, l2b_ref, past_ref, sel1_ref, sel2_ref, y_ref, a_ref,
         a_scr, x1_scr, x1b_scr, g_scr, h_scr, acc_scr) = refs
    else:
        (x_ref, og_ref, od_ref, wg_ref, wo_ref, l1g_ref, l1b_ref, wup_ref, cw_ref, cb_ref, wdn_ref,
         l2g_ref, l2b_ref, y_ref, a_ref,
         a_scr, x1_scr, x1b_scr, g_scr, h_scr, acc_scr) = refs
    d = x_ref.shape[1]
    pad = SUBLANES
    if sample:
        a_scr[0:pad, :] = jnp.zeros((pad, d_ff), F32)
        r = lax.broadcasted_iota(jnp.int32, (tm, fc), 0) % seg
    else:
        first = pl.program_id(0) % blocks_per_seq == 0

        @pl.when(first)
        def _():
            a_scr[0:pad, :] = jnp.zeros((pad, d_ff), F32)

        @pl.when(jnp.logical_not(first))
        def _():
            a_scr[0:pad, :] = a_scr[tm:tm + pad, :]

    xb_scr, mix_scr = x1b_scr, acc_scr
    xb_scr[...] = x_ref[...].astype(BF16)
    gate_a = jnp.dot(xb_scr[...], wg_ref[:, 0:d], preferred_element_type=F32)
    mix_scr[...] = _sigmoid(gate_a) * og_ref[...]
    gate_b = jnp.dot(xb_scr[...], wg_ref[:, d:2 * d], preferred_element_type=F32)
    xb_scr[...] = (mix_scr[...] + _sigmoid(gate_b) * od_ref[...]).astype(BF16)
    x1 = _layer_norm(alpha * x_ref[...] + jnp.dot(xb_scr[...], wo_ref[...], preferred_element_type=F32),
                     l1g_ref[...], l1b_ref[...])
    x1_scr[...] = x1
    x1b_scr[...] = x1.astype(BF16)

    def up(c):
        a_scr[pad:pad + tm, c * fc:(c + 1) * fc] = jnp.dot(
            x1b_scr[...], wup_ref[:, c * fc:(c + 1) * fc], preferred_element_type=F32)
        g_scr[c % 2] = jnp.dot(x1b_scr[...], wup_ref[:, d_ff + c * fc:d_ff + (c + 1) * fc],
                               preferred_element_type=F32)

    n_chunks = d_ff // fc
    up(0)
    for c in range(n_chunks):
        cs = slice(c * fc, (c + 1) * fc)
        if c + 1 < n_chunks:
            up(c + 1)
        a = a_scr[pad:pad + tm, cs]
        gate = g_scr[c % 2]
        a1 = a_scr[pad - 1:pad - 1 + tm, cs]
        a2 = a_scr[pad - 2:pad - 2 + tm, cs]
        if sample:
            past = past_ref[:, cs]
            hist1 = jnp.dot(sel1_ref[...], past, precision=lax.Precision.HIGHEST, preferred_element_type=F32)
            hist2 = jnp.dot(sel2_ref[...], past, precision=lax.Precision.HIGHEST, preferred_element_type=F32)
            a1 = jnp.where(r == 0, 0.0, a1) + hist1
            a2 = jnp.where(r < 2, 0.0, a2) + hist2
        conv = cb_ref[:, cs] + a2 * cw_ref[0:1, cs] + a1 * cw_ref[1:2, cs] + a * cw_ref[2:3, cs]
        h_scr[c % 2] = (_gelu_tanh(conv) * gate).astype(BF16)
        down = jnp.dot(h_scr[c % 2], wdn_ref[cs, :], preferred_element_type=F32)
        if c == 0:
            acc_scr[...] = down
        else:
            acc_scr[...] += down

    if sample:
        a_ref[...] = a_scr[pad:pad + tm, :]
    else:
        a_ref[0] = a_scr[tm:tm + pad, :]
    y_ref[...] = _layer_norm(alpha * x1_scr[...] + acc_scr[...], l2g_ref[...], l2b_ref[...])


def _ffn(x, og, od, w_gate, w_out, l1g, l1b, w_up, conv_w, conv_b, w_down, l2g, l2b,
         *, tm, alpha, seq_len, past=None, fc=256):
    rows, d = x.shape
    d_ff = w_down.shape[0]
    tm = min(tm, rows)
    nblk = rows // tm
    sample = past is not None
    row_spec = pl.BlockSpec((tm, d), lambda i: (i, 0))
    weights = [w_gate, w_out, l1g, l1b, w_up, conv_w, conv_b, w_down, l2g, l2b]
    args = [x, og, od, *weights]
    scratch = [pltpu.VMEM((tm + SUBLANES, d_ff), F32), pltpu.VMEM((tm, d), F32), pltpu.VMEM((tm, d), BF16),
               pltpu.VMEM((2, tm, fc), F32), pltpu.VMEM((2, tm, fc), BF16), pltpu.VMEM((tm, d), F32)]
    if sample:
        nseq = tm // seq_len
        sel1 = np.zeros((tm, past.shape[0]), np.float32)
        sel2 = np.zeros((tm, past.shape[0]), np.float32)
        for s in range(nseq):
            sel1[s * seq_len, 2 * s + 1] = 1.0
            sel2[s * seq_len, 2 * s] = 1.0
            sel2[s * seq_len + 1, 2 * s + 1] = 1.0
        assert nblk == 1
        args += [past, jnp.asarray(sel1), jnp.asarray(sel2)]
        a_shape = jax.ShapeDtypeStruct((rows, d_ff), F32)
        a_spec = pl.BlockSpec((tm, d_ff), lambda i: (i, 0))
        kern = functools.partial(_ffn_kernel, tm=tm, d_ff=d_ff, fc=fc, alpha=alpha, blocks_per_seq=None, seg=seq_len)
    else:
        a_shape = jax.ShapeDtypeStruct((nblk, SUBLANES, d_ff), F32)
        a_spec = pl.BlockSpec((1, SUBLANES, d_ff), lambda i: (i, 0, 0))
        kern = functools.partial(_ffn_kernel, tm=tm, d_ff=d_ff, fc=fc, alpha=alpha,
                                 blocks_per_seq=seq_len // tm, seg=None)
    in_specs = [row_spec, row_spec, row_spec] + [_resident(a.shape) for a in args[3:]]
    return pl.pallas_call(
        kern,
        grid=(nblk,),
        in_specs=in_specs,
        out_specs=[row_spec, a_spec],
        out_shape=[jax.ShapeDtypeStruct((rows, d), F32), a_shape],
        scratch_shapes=scratch,
        compiler_params=_params(("arbitrary",)),
        name="ffn",
    )(*args)


def kernel(x_prompt, x_sample, cache_diff_k, cache_diff_v, state_gla, cache_ffn_conv, w_in, w_a2, b_a, gla_norm_g, lam_q1, lam_k1, lam_q2, lam_k2, diff_norm_g, w_out, ln1_g, ln1_b, w_up, conv_w, conv_b, w_down, ln2_g, ln2_b):
    depth = w_in.shape[0]
    assert depth == 1, "single-layer step"
    nbp, tp, d = x_prompt.shape
    nbs, ts, _ = x_sample.shape
    past_len = cache_diff_k.shape[2]
    qk = w_a2.shape[2]
    vd = GLA_HEADS * gla_norm_g.shape[1]
    dqk = cache_diff_v.shape[3] * cache_diff_v.shape[4]
    d_ff = w_down.shape[1]
    alpha = (2.0 * depth) ** 0.25
    lam_init = 0.8 - 0.6 * math.exp(-0.3 * 0)

    w = w_in[0]
    o_low = 2 * qk + 2 * vd
    o_diff = o_low + GLA_RANK
    o_gate = o_diff + 3 * dqk
    w_gla = w[:, :o_low].astype(BF16)
    w_low = jnp.pad(w[:, o_low:o_diff], ((0, 0), (0, LANES - GLA_RANK))).astype(BF16)
    w_diff = w[:, o_diff:o_gate].astype(BF16)
    w_gate = w[:, o_gate:].astype(BF16)
    w_a2p = jnp.pad(w_a2[0], ((0, LANES - GLA_RANK), (0, 0))).astype(BF16)
    row2 = lambda a: a.reshape(1, -1)
    lams = [row2(lam_q1[0]), row2(lam_k1[0]), row2(lam_q2[0]), row2(lam_k2[0])]
    ffn_w = (w_gate, w_out[0].astype(BF16), row2(ln1_g[0]), row2(ln1_b[0]), w_up[0].astype(BF16), conv_w[0],
             row2(conv_b[0]), w_down[0].astype(BF16), row2(ln2_g[0]), row2(ln2_b[0]))
    gn = row2(gla_norm_g[0])
    dg = row2(diff_norm_g[0])
    proj = functools.partial(_proj, w_gla=w_gla, w_low=w_low, w_a2=w_a2p, b_a=row2(b_a[0]), w_diff=w_diff,
                             qk=qk, vd=vd, dqk=dqk, tm=256)

    xp = x_prompt.reshape(nbp * tp, d)
    gq, gk, gv, gg, la, dq, dk_p, dv_p, kb, vb = proj(xp)
    og, state_p = _gla(gq, gk, gv, gg, la, gn, None, nb=nbp, t=tp, c=128, nsub=2)
    od = _attn_prompt(dq, kb, vb, lams, dg, nb=nbp, t=tp, tq=512, tk=512, lam_init=lam_init)
    y_p, a_last = _ffn(xp, og, od, *ffn_w, tm=256, alpha=alpha, seq_len=tp)
    conv_p = a_last.reshape(nbp, -1, SUBLANES, d_ff)[:, -1, SUBLANES - (CONV_W - 1):, :]

    xs = x_sample.reshape(nbs * ts, d)
    gq, gk, gv, gg, la, dq, dk_s, dv_s, kb, vb = proj(xs)
    og, state_s = _gla(gq, gk, gv, gg, la, gn, state_gla[0], nb=nbs, t=ts, c=ts)
    od = _attn_sample(dq, kb, vb, cache_diff_k[0].reshape(nbs, past_len, dqk),
                      cache_diff_v[0].reshape(nbs, past_len, dqk), lams, dg, nb=nbs, t=ts, lam_init=lam_init)
    y_s, a_s = _ffn(xs, og, od, *ffn_w, tm=nbs * ts, alpha=alpha, seq_len=ts,
                    past=cache_ffn_conv[0].reshape(nbs * (CONV_W - 1), d_ff))
    conv_s = a_s.reshape(nbs, ts, d_ff)[:, ts - (CONV_W - 1):, :]

    heads = dqk // (2 * DIFF_HEAD_DIM)
    k_shape = lambda nb, t: (1, nb, t, heads, 2, DIFF_HEAD_DIM)
    v_shape = lambda nb, t: (1, nb, t, heads, 2 * DIFF_HEAD_DIM)
    return (y_p.reshape(nbp, tp, d), y_s.reshape(nbs, ts, d),
            dk_p.reshape(k_shape(nbp, tp)), dv_p.reshape(v_shape(nbp, tp)), state_p[None], conv_p[None],
            dk_s.reshape(k_shape(nbs, ts)), dv_s.reshape(v_shape(nbs, ts)), state_s[None], conv_s[None])
```

```python
import functools
import math

import numpy as np
import jax
import jax.numpy as jnp
from jax import lax
from jax.experimental import pallas as pl
from jax.experimental.pallas import tpu as pltpu

F32 = jnp.float32
BF16 = jnp.bfloat16

CHUNK = 64
GLA_HEADS = 4
GLA_RANK = 16
GLA_TAU = 16.0
GLA_BLOCK = 16
DIFF_HEAD_DIM = 64
CONV_W = 3
EPS = 1e-5
LOG2_E = math.log2(math.e)
TILE_GROUP = 4

LANES = 128
SUBLANES = 8
VMEM_LIMIT = 56 * 1024 * 1024

NT_DIMS = (((1,), (1,)), ((), ()))
TN_DIMS = (((0,), (0,)), ((), ()))


def _sigmoid(x):
    return 1.0 / (1.0 + jnp.exp(-x))


def _layer_norm(x, g, b):
    mu = jnp.mean(x, -1, keepdims=True)
    xc = x - mu
    var = jnp.mean(xc * xc, -1, keepdims=True)
    return xc * lax.rsqrt(var + EPS) * g + b


def _resident(shape):
    nd = len(shape)
    return pl.BlockSpec(shape, lambda *_: (0,) * nd, pipeline_mode=pl.Buffered(1))


def _params(sem):
    return pltpu.CompilerParams(dimension_semantics=sem, vmem_limit_bytes=VMEM_LIMIT)


def _proj_kernel(x_ref, wgla_ref, wlow_ref, wa2_ref, ba_ref, wq_ref, wk_ref, wv_ref,
                 gq_ref, gk_ref, gv_ref, gg_ref, la_ref, dq_ref, dk_ref, dv_ref, kb_ref, vb_ref,
                 *, qk, vd, dqk, k_transposed):
    xb = x_ref[...].astype(BF16)

    def mm(w_ref, lo, hi):
        return jnp.dot(xb, w_ref[:, lo:hi], preferred_element_type=F32)

    gq_ref[...] = mm(wgla_ref, 0, qk)
    gk_ref[...] = mm(wgla_ref, qk, 2 * qk)
    gv_ref[...] = mm(wgla_ref, 2 * qk, 2 * qk + vd)
    gg_ref[...] = mm(wgla_ref, 2 * qk + vd, 2 * qk + 2 * vd)

    g_low = jnp.dot(xb, wlow_ref[...], preferred_element_type=F32)
    z = jnp.dot(g_low.astype(BF16), wa2_ref[...], preferred_element_type=F32) + ba_ref[...]
    softplus_neg = jnp.maximum(-z, 0.0) + jnp.log1p(jnp.exp(-jnp.abs(z)))
    la_ref[...] = -softplus_neg * (1.0 / GLA_TAU)

    dq_ref[...] = (mm(wq_ref, 0, dqk) * (DIFF_HEAD_DIM ** -0.5 * LOG2_E)).astype(BF16)
    if k_transposed:
        dk = lax.dot_general(wk_ref[...], xb, NT_DIMS, preferred_element_type=F32)
        dk_ref[0] = dk
        kb_ref[0] = dk.astype(BF16)
    else:
        dk = mm(wk_ref, 0, dqk)
        dk_ref[...] = dk
        kb_ref[...] = dk.astype(BF16)
    dv = mm(wv_ref, 0, dqk)
    dv_ref[...] = dv
    vb_ref[...] = dv.astype(BF16)


def _proj(x, w_gla, w_low, w_a2, b_a, w_q, w_k, w_v, *, qk, vd, dqk, tm, kt_frames=None):
    rows, d = x.shape
    tm = min(tm, rows)
    row_spec = lambda n: pl.BlockSpec((tm, n), lambda i: (i, 0))
    out = lambda n, dt: jax.ShapeDtypeStruct((rows, n), dt)
    if kt_frames is None:
        k_spec, k_out = row_spec(dqk), lambda dt: out(dqk, dt)
    else:
        per_seq = kt_frames // tm
        k_spec = pl.BlockSpec((1, dqk, tm), lambda i: (i // per_seq, 0, i % per_seq))
        k_out = lambda dt: jax.ShapeDtypeStruct((rows // kt_frames, dqk, kt_frames), dt)
    weights = (w_gla, w_low, w_a2, b_a, w_q, w_k, w_v)
    return pl.pallas_call(
        functools.partial(_proj_kernel, qk=qk, vd=vd, dqk=dqk, k_transposed=kt_frames is not None),
        grid=(rows // tm,),
        in_specs=[row_spec(d)] + [_resident(a.shape) for a in weights],
        out_specs=[row_spec(qk), row_spec(qk), row_spec(vd), row_spec(vd), row_spec(qk),
                   row_spec(dqk), k_spec, row_spec(dqk), k_spec, row_spec(dqk)],
        out_shape=[out(qk, F32), out(qk, F32), out(vd, F32), out(vd, F32), out(qk, F32),
                   out(dqk, BF16), k_out(F32), out(dqk, F32), k_out(BF16), out(dqk, BF16)],
        compiler_params=_params(("parallel",)),
        name="proj",
    )(x, *weights)


def _gla_levels(c):
    d = min(c, 2 * GLA_BLOCK)
    off = []
    b = c // 2
    while b >= d:
        off.append(b)
        b //= 2
    return off, d


def _row_broadcast_refs(cum, group, ref_offset):
    c, n = cum.shape
    parts = [jnp.broadcast_to(cum[p * group + ref_offset:p * group + ref_offset + 1, :], (group, n))
             for p in range(c // group)]
    return parts[0] if len(parts) == 1 else jnp.concatenate(parts, axis=0)


def _gla_kernel(*refs, c, nsub, dk, dv, has_s0):
    if has_s0:
        q_ref, k_ref, v_ref, gg_ref, la_ref, gn_ref, s0_ref, o_ref, sout_ref, s_scr = refs
    else:
        q_ref, k_ref, v_ref, gg_ref, la_ref, gn_ref, o_ref, sout_ref, s_scr = refs
    step = pl.program_id(1)

    @pl.when(step == 0)
    def _():
        s_scr[...] = s0_ref[0] if has_s0 else jnp.zeros_like(s_scr)

    for sub in range(nsub):
        rows = slice(sub * c, (sub + 1) * c)
        _gla_chunk(q_ref, k_ref, v_ref, gg_ref, la_ref, gn_ref, o_ref, s_scr, rows, c=c, dk=dk, dv=dv)

    @pl.when(step == pl.num_programs(1) - 1)
    def _():
        sout_ref[0] = s_scr[...]


def _gla_chunk(q_ref, k_ref, v_ref, gg_ref, la_ref, gn_ref, o_ref, s_scr, rows, *, c, dk, dv):
    row = lax.broadcasted_iota(jnp.int32, (c, c), 0)
    col = lax.broadcasted_iota(jnp.int32, (c, c), 1)
    tri = (col <= row).astype(F32)
    cum_all = jnp.dot(tri, la_ref[rows, :], precision=lax.Precision.HIGHEST, preferred_element_type=F32)

    off_levels, d = _gla_levels(c)
    scale = dk ** -0.5
    gn = gn_ref[...]

    for h in range(GLA_HEADS):
        ks = slice(h * dk, (h + 1) * dk)
        vs = slice(h * dv, (h + 1) * dv)
        cum = cum_all[:, ks]
        qs = q_ref[rows, ks] * scale
        kk = k_ref[rows, ks]
        vv = v_ref[rows, vs].astype(BF16)

        ref = _row_broadcast_refs(cum, d, d // 2 - 1)
        qd = (qs * jnp.exp(cum - ref)).astype(BF16)
        kd = (kk * jnp.exp(ref - cum)).astype(BF16)
        att = lax.dot_general(qd, kd, NT_DIMS, preferred_element_type=F32)
        att = jnp.where((row // d == col // d) & (col <= row), att, 0.0)
        for b in off_levels:
            ref = _row_broadcast_refs(cum, 2 * b, b - 1)
            ql = (qs * jnp.exp(jnp.minimum(cum - ref, 0.0))).astype(BF16)
            kl = (kk * jnp.exp(jnp.minimum(ref - cum, 0.0))).astype(BF16)
            a_l = lax.dot_general(ql, kl, NT_DIMS, preferred_element_type=F32)
            att = jnp.where(((row // b) % 2 == 1) & (col // b == row // b - 1), a_l, att)

        s_h = s_scr[h]
        q_dec = (qs * jnp.exp(cum)).astype(BF16)
        o = (jnp.dot(att.astype(BF16), vv, preferred_element_type=F32)
             + jnp.dot(q_dec, s_h.astype(BF16), preferred_element_type=F32))

        last = cum[c - 1:c, :]
        k_end = (kk * jnp.exp(last - cum)).astype(BF16)
        kv = lax.dot_general(k_end, vv, TN_DIMS, preferred_element_type=F32)
        decay_col = jnp.transpose(jnp.broadcast_to(jnp.exp(last), (dk, dk)))
        decay = jnp.concatenate([decay_col] * (dv // dk), axis=1)
        s_scr[h] = decay * s_h + kv

        ms = jnp.mean(o * o, -1, keepdims=True)
        gate = gg_ref[rows, vs]
        o_ref[rows, vs] = o * lax.rsqrt(ms + EPS) * gn * (gate * _sigmoid(gate))


def _gla(gq, gk, gv, gg, la, gn, s0, *, nb, t, c, nsub=1):
    rows, qk = gq.shape
    vd = gv.shape[1]
    dk, dv = qk // GLA_HEADS, vd // GLA_HEADS
    c = min(c, t)
    nsub = min(nsub, t // c)
    nc = t // (c * nsub)
    blk = lambda n: pl.BlockSpec((c * nsub, n), lambda b, s: (b * nc + s, 0))
    state_spec = pl.BlockSpec((1, GLA_HEADS, dk, dv), lambda b, s: (b, 0, 0, 0))
    in_specs = [blk(qk), blk(qk), blk(vd), blk(vd), blk(qk), _resident(gn.shape)]
    args = [gq, gk, gv, gg, la, gn]
    if s0 is not None:
        in_specs.append(state_spec)
        args.append(s0)
    return pl.pallas_call(
        functools.partial(_gla_kernel, c=c, nsub=nsub, dk=dk, dv=dv, has_s0=s0 is not None),
        grid=(nb, nc),
        in_specs=in_specs,
        out_specs=[blk(vd), state_spec],
        out_shape=[jax.ShapeDtypeStruct((rows, vd), F32),
                   jax.ShapeDtypeStruct((nb, GLA_HEADS, dk, dv), F32)],
        scratch_shapes=[pltpu.VMEM((GLA_HEADS, dk, dv), F32)],
        compiler_params=_params(("parallel", "arbitrary")),
        name="gla",
    )(*args)


def _stack_half_queries(q):
    lane = lax.broadcasted_iota(jnp.int32, q.shape, 1)
    zero = jnp.zeros_like(q)
    return jnp.concatenate([jnp.where(lane < DIFF_HEAD_DIM, q, zero),
                            jnp.where(lane >= DIFF_HEAD_DIM, q, zero)], axis=0)


def _lambda(lq1_ref, lk1_ref, lq2_ref, lk2_ref, lam_init):
    return (jnp.exp(jnp.sum(lq1_ref[...] * lk1_ref[...], -1, keepdims=True))
            - jnp.exp(jnp.sum(lq2_ref[...] * lk2_ref[...], -1, keepdims=True)) + lam_init)


def _diff_finish(acc, l, lam, g, lam_init, tq):
    o = acc * (1.0 / l)
    od = o[:tq] - lam * o[tq:]
    ms = jnp.mean(od * od, -1, keepdims=True)
    return od * lax.rsqrt(ms + EPS) * g * (1.0 - lam_init)


def _attn_prompt_kernel(q_ref, k_ref, v_ref, lq1_ref, lk1_ref, lq2_ref, lk2_ref, g_ref, o_ref,
                        qs_scr, s_scr, p_scr, m_scr, l_scr, a_scr, acc_scr, *, tq, tk, lam_init):
    i = pl.program_id(2)
    q_first = i * tq
    qs_scr[...] = _stack_half_queries(q_ref[...])
    m_scr[...] = jnp.full(m_scr.shape, -jnp.inf, F32)
    l_scr[...] = jnp.zeros(l_scr.shape, F32)
    acc_scr[...] = jnp.zeros(acc_scr.shape, F32)
    reps = tk // LANES

    def tile(j, carry, diagonal):
        start = pl.multiple_of(j * tk, tk)
        s_scr[...] = jnp.dot(qs_scr[...], k_ref[:, pl.ds(start, tk)],
                             preferred_element_type=F32)
        for r0 in range(0, 2 * tq, CHUNK):
            rows = slice(r0, r0 + CHUNK)
            limit = ((r0 % tq) // CHUNK + 1) * CHUNK if diagonal else tk
            pieces = []
            for g in range(reps):
                lo = g * LANES
                if lo >= limit:
                    pieces.append(None)
                elif lo + LANES <= limit:
                    pieces.append(s_scr[rows, lo:lo + LANES])
                else:
                    lane = lax.broadcasted_iota(jnp.int32, (CHUNK, LANES), 1)
                    pieces.append(jnp.where(lane < limit - lo, s_scr[rows, lo:lo + LANES], -jnp.inf))
            seen = [x for x in pieces if x is not None]
            m_prev = m_scr[rows, :]
            m_next = jnp.maximum(m_prev, jnp.max(functools.reduce(jnp.maximum, seen), -1, keepdims=True))
            alpha = jnp.exp2(m_prev - m_next)
            p_lanes = None
            for g, x in enumerate(pieces):
                lo = g * LANES
                if x is None:
                    p_scr[rows, lo:lo + LANES] = jnp.zeros((CHUNK, LANES), BF16)
                    continue
                p = jnp.exp2(x - m_next)
                p_lanes = p if p_lanes is None else p_lanes + p
                p_scr[rows, lo:lo + LANES] = p.astype(BF16)
            l_scr[rows, :] = alpha * l_scr[rows, :] + p_lanes
            m_scr[rows, :] = m_next
            a_scr[rows, :] = alpha
        acc_scr[...] = a_scr[...] * acc_scr[...] + jnp.dot(p_scr[...], v_ref[pl.ds(start, tk), :],
                                                           preferred_element_type=F32)
        return carry

    n_full = q_first // tk
    def tile_group(jj, carry):
        for u in range(TILE_GROUP):
            tile(TILE_GROUP * jj + u, carry, diagonal=False)
        return carry

    lax.fori_loop(0, n_full // TILE_GROUP, tile_group, 0)
    lax.fori_loop(TILE_GROUP * (n_full // TILE_GROUP), n_full, functools.partial(tile, diagonal=False), 0)
    tile(n_full, 0, diagonal=True)

    lam = _lambda(lq1_ref, lk1_ref, lq2_ref, lk2_ref, lam_init)
    l = jnp.sum(l_scr[...], -1, keepdims=True)
    o_ref[...] = _diff_finish(acc_scr[...], l, lam, g_ref[...], lam_init, tq)


def _attn_prompt(dq, kb, vb, lams, g, *, nb, t, tile, lam_init):
    rows, dqk = dq.shape
    hw = 2 * DIFF_HEAD_DIM
    heads = dqk // hw
    tq = tk = min(tile, t)
    assert tq % CHUNK == 0 and t % tq == 0
    nq = t // tq
    k_spec = pl.BlockSpec((None, hw, t), lambda b, h, i: (b, h, 0))
    v_spec = pl.BlockSpec((t, hw), lambda b, h, i: (b, h))
    q_spec = pl.BlockSpec((tq, hw), lambda b, h, i: (b * nq + i, h))
    small = [_resident(a.shape) for a in (*lams, g)]
    stat = pltpu.VMEM((2 * tq, LANES), F32)
    return pl.pallas_call(
        functools.partial(_attn_prompt_kernel, tq=tq, tk=tk, lam_init=lam_init),
        grid=(nb, heads, nq),
        in_specs=[q_spec, k_spec, v_spec, *small],
        out_specs=q_spec,
        out_shape=jax.ShapeDtypeStruct((rows, dqk), F32),
        scratch_shapes=[pltpu.VMEM((2 * tq, hw), BF16), pltpu.VMEM((2 * tq, tk), F32),
                        pltpu.VMEM((2 * tq, tk), BF16), stat, stat, stat, pltpu.VMEM((2 * tq, hw), F32)],
        compiler_params=_params(("parallel", "parallel", "arbitrary")),
        name="attn_prompt",
    )(dq, kb, vb, *lams, g)


def _attn_sample_kernel(q_ref, kn_ref, vn_ref, kc_ref, vc_ref, lq1_ref, lk1_ref, lq2_ref, lk2_ref, g_ref,
                        o_ref, *, heads, lam_init):
    tq = q_ref.shape[0]
    hw = 2 * DIFF_HEAD_DIM
    past = kc_ref.shape[2]
    lam = _lambda(lq1_ref, lk1_ref, lq2_ref, lk2_ref, lam_init)
    g = g_ref[...]
    for h in range(heads):
        hs = slice(h * hw, (h + 1) * hw)
        qs = _stack_half_queries(q_ref[:, hs])
        k_past = kc_ref[0, hs, :].astype(BF16)
        v_past = vc_ref[0, pl.ds(h, past, stride=heads), :].astype(BF16)
        s_past = jnp.dot(qs, k_past, preferred_element_type=F32)
        s_new = lax.dot_general(qs, kn_ref[:, hs], NT_DIMS, preferred_element_type=F32)
        m = jnp.maximum(jnp.max(s_past, -1, keepdims=True), jnp.max(s_new, -1, keepdims=True))
        p_past = jnp.exp2(s_past - m)
        p_new = jnp.exp2(s_new - m)
        l = jnp.sum(p_past, -1, keepdims=True) + jnp.sum(p_new, -1, keepdims=True)
        acc = (jnp.dot(p_past.astype(BF16), v_past, preferred_element_type=F32)
               + jnp.dot(p_new.astype(BF16), vn_ref[:, hs], preferred_element_type=F32))
        o_ref[:, hs] = _diff_finish(acc, l, lam, g, lam_init, tq)


def _attn_sample(dq, kb, vb, k_cache, v_cache, lams, g, *, nb, t, lam_init):
    rows, dqk = dq.shape
    hw = 2 * DIFF_HEAD_DIM
    heads = dqk // hw
    past = k_cache.shape[2]
    q_pos = past + np.arange(t)
    assert ((q_pos // CHUNK + 1) * CHUNK >= past + t).all(), "sample queries must see every cached and new key"
    new_spec = pl.BlockSpec((t, dqk), lambda b: (b, 0))
    k_cache_spec = pl.BlockSpec((1, dqk, past), lambda b: (b, 0, 0))
    v_cache_spec = pl.BlockSpec((1, past * heads, hw), lambda b: (b, 0, 0))
    small = [_resident(a.shape) for a in (*lams, g)]
    return pl.pallas_call(
        functools.partial(_attn_sample_kernel, heads=heads, lam_init=lam_init),
        grid=(nb,),
        in_specs=[new_spec, new_spec, new_spec, k_cache_spec, v_cache_spec, *small],
        out_specs=new_spec,
        out_shape=jax.ShapeDtypeStruct((rows, dqk), F32),
        compiler_params=_params(("parallel",)),
        name="attn_sample",
    )(dq, kb, vb, k_cache, v_cache, *lams, g)


def _gelu_tanh(x):
    return x * (0.5 * (1.0 + jnp.tanh(math.sqrt(2.0 / math.pi) * (x + 0.044715 * (x * x * x)))))


def _ffn_kernel(*refs, tm, d_ff, fc, alpha, blocks_per_seq, seg):
    sample = seg is not None
    if sample:
        (x_ref, og_ref, od_ref, wg_ref, wo_ref, l1g_ref, l1b_ref, wup_ref, cw_ref, cb_ref, wdn_ref,
         l2g_ref, l2b_ref, past_ref, sel1_ref, sel2_ref, y_ref, a_ref,
         a_scr, x1_scr, x1b_scr, g_scr, h_scr, acc_scr) = refs
    else:
        (x_ref, og_ref, od_ref, wg_ref, wo_ref, l1g_ref, l1b_ref, wup_ref, cw_ref, cb_ref, wdn_ref,
         l2g_ref, l2b_ref, y_ref, a_ref,
         a_scr, x1_scr, x1b_scr, g_scr, h_scr, acc_scr) = refs
    d = x_ref.shape[1]
    pad = SUBLANES
    if sample:
        a_scr[0:pad, :] = jnp.zeros((pad, d_ff), F32)
        r = lax.broadcasted_iota(jnp.int32, (tm, fc), 0) % seg
    else:
        first = pl.program_id(0) % blocks_per_seq == 0

        @pl.when(first)
        def _():
            a_scr[0:pad, :] = jnp.zeros((pad, d_ff), F32)

        @pl.when(jnp.logical_not(first))
        def _():
            a_scr[0:pad, :] = a_scr[tm:tm + pad, :]

    xb_scr, mix_scr = x1b_scr, acc_scr
    xb_scr[...] = x_ref[...].astype(BF16)
    gate_a = jnp.dot(xb_scr[...], wg_ref[:, 0:d], preferred_element_type=F32)
    mix_scr[...] = _sigmoid(gate_a) * og_ref[...]
    gate_b = jnp.dot(xb_scr[...], wg_ref[:, d:2 * d], preferred_element_type=F32)
    xb_scr[...] = (mix_scr[...] + _sigmoid(gate_b) * od_ref[...]).astype(BF16)
    x1 = _layer_norm(alpha * x_ref[...] + jnp.dot(xb_scr[...], wo_ref[...], preferred_element_type=F32),
                     l1g_ref[...], l1b_ref[...])
    x1_scr[...] = x1
    x1b_scr[...] = x1.astype(BF16)

    def up(c):
        a_scr[pad:pad + tm, c * fc:(c + 1) * fc] = jnp.dot(
            x1b_scr[...], wup_ref[:, c * fc:(c + 1) * fc], preferred_element_type=F32)
        g_scr[c % 2] = jnp.dot(x1b_scr[...], wup_ref[:, d_ff + c * fc:d_ff + (c + 1) * fc],
                               preferred_element_type=F32)

    n_chunks = d_ff // fc
    up(0)
    for c in range(n_chunks):
        cs = slice(c * fc, (c + 1) * fc)
        if c + 1 < n_chunks:
            up(c + 1)
        a = a_scr[pad:pad + tm, cs]
        gate = g_scr[c % 2]
        a1 = a_scr[pad - 1:pad - 1 + tm, cs]
        a2 = a_scr[pad - 2:pad - 2 + tm, cs]
        if sample:
            past = past_ref[:, cs]
            hist1 = jnp.dot(sel1_ref[...], past, precision=lax.Precision.HIGHEST, preferred_element_type=F32)
            hist2 = jnp.dot(sel2_ref[...], past, precision=lax.Precision.HIGHEST, preferred_element_type=F32)
            a1 = jnp.where(r == 0, 0.0, a1) + hist1
            a2 = jnp.where(r < 2, 0.0, a2) + hist2
        conv = cb_ref[:, cs] + a2 * cw_ref[0:1, cs] + a1 * cw_ref[1:2, cs] + a * cw_ref[2:3, cs]
        h_scr[c % 2] = (_gelu_tanh(conv) * gate).astype(BF16)
        down = jnp.dot(h_scr[c % 2], wdn_ref[cs, :], preferred_element_type=F32)
        if c == 0:
            acc_scr[...] = down
        else:
            acc_scr[...] += down

    if sample:
        a_ref[...] = a_scr[pad:pad + tm, :]
    else:
        a_ref[0] = a_scr[tm:tm + pad, :]
    y_ref[...] = _layer_norm(alpha * x1_scr[...] + acc_scr[...], l2g_ref[...], l2b_ref[...])


def _ffn(x, og, od, w_gate, w_out, l1g, l1b, w_up, conv_w, conv_b, w_down, l2g, l2b,
         *, tm, alpha, seq_len, past=None, fc=256):
    rows, d = x.shape
    d_ff = w_down.shape[0]
    tm = min(tm, rows)
    nblk = rows // tm
    sample = past is not None
    row_spec = pl.BlockSpec((tm, d), lambda i: (i, 0))
    weights = [w_gate, w_out, l1g, l1b, w_up, conv_w, conv_b, w_down, l2g, l2b]
    args = [x, og, od, *weights]
    scratch = [pltpu.VMEM((tm + SUBLANES, d_ff), F32), pltpu.VMEM((tm, d), F32), pltpu.VMEM((tm, d), BF16),
               pltpu.VMEM((2, tm, fc), F32), pltpu.VMEM((2, tm, fc), BF16), pltpu.VMEM((tm, d), F32)]
    if sample:
        nseq = tm // seq_len
        sel1 = np.zeros((tm, past.shape[0]), np.float32)
        sel2 = np.zeros((tm, past.shape[0]), np.float32)
        for s in range(nseq):
            sel1[s * seq_len, 2 * s + 1] = 1.0
            sel2[s * seq_len, 2 * s] = 1.0
            sel2[s * seq_len + 1, 2 * s + 1] = 1.0
        assert nblk == 1
        args += [past, jnp.asarray(sel1), jnp.asarray(sel2)]
        a_shape = jax.ShapeDtypeStruct((rows, d_ff), F32)
        a_spec = pl.BlockSpec((tm, d_ff), lambda i: (i, 0))
        kern = functools.partial(_ffn_kernel, tm=tm, d_ff=d_ff, fc=fc, alpha=alpha, blocks_per_seq=None, seg=seq_len)
    else:
        a_shape = jax.ShapeDtypeStruct((nblk, SUBLANES, d_ff), F32)
        a_spec = pl.BlockSpec((1, SUBLANES, d_ff), lambda i: (i, 0, 0))
        kern = functools.partial(_ffn_kernel, tm=tm, d_ff=d_ff, fc=fc, alpha=alpha,
                                 blocks_per_seq=seq_len // tm, seg=None)
    in_specs = [row_spec, row_spec, row_spec] + [_resident(a.shape) for a in args[3:]]
    return pl.pallas_call(
        kern,
        grid=(nblk,),
        in_specs=in_specs,
        out_specs=[row_spec, a_spec],
        out_shape=[jax.ShapeDtypeStruct((rows, d), F32), a_shape],
        scratch_shapes=scratch,
        compiler_params=_params(("arbitrary",)),
        name="ffn",
    )(*args)


def kernel(x_prompt, x_sample, cache_diff_k, cache_diff_v, state_gla, cache_ffn_conv, w_in, w_a2, b_a, gla_norm_g, lam_q1, lam_k1, lam_q2, lam_k2, diff_norm_g, w_out, ln1_g, ln1_b, w_up, conv_w, conv_b, w_down, ln2_g, ln2_b):
    depth = w_in.shape[0]
    assert depth == 1, "single-layer step"
    nbp, tp, d = x_prompt.shape
    nbs, ts, _ = x_sample.shape
    past_len = cache_diff_k.shape[2]
    qk = w_a2.shape[2]
    vd = GLA_HEADS * gla_norm_g.shape[1]
    dqk = cache_diff_v.shape[3] * cache_diff_v.shape[4]
    d_ff = w_down.shape[1]
    alpha = (2.0 * depth) ** 0.25
    lam_init = 0.8 - 0.6 * math.exp(-0.3 * 0)

    w = w_in[0]
    o_low = 2 * qk + 2 * vd
    o_diff = o_low + GLA_RANK
    o_gate = o_diff + 3 * dqk
    w_gla = w[:, :o_low].astype(BF16)
    w_low = jnp.pad(w[:, o_low:o_diff], ((0, 0), (0, LANES - GLA_RANK))).astype(BF16)
    w_q = w[:, o_diff:o_diff + dqk].astype(BF16)
    w_k = w[:, o_diff + dqk:o_diff + 2 * dqk].astype(BF16)
    w_v = w[:, o_diff + 2 * dqk:o_gate].astype(BF16)
    w_gate = w[:, o_gate:].astype(BF16)
    w_a2p = jnp.pad(w_a2[0], ((0, LANES - GLA_RANK), (0, 0))).astype(BF16)
    row2 = lambda a: a.reshape(1, -1)
    lams = [row2(lam_q1[0]), row2(lam_k1[0]), row2(lam_q2[0]), row2(lam_k2[0])]
    ffn_w = (w_gate, w_out[0].astype(BF16), row2(ln1_g[0]), row2(ln1_b[0]), w_up[0].astype(BF16), conv_w[0],
             row2(conv_b[0]), w_down[0].astype(BF16), row2(ln2_g[0]), row2(ln2_b[0]))
    gn = row2(gla_norm_g[0])
    dg = row2(diff_norm_g[0])
    proj = functools.partial(_proj, w_gla=w_gla, w_low=w_low, w_a2=w_a2p, b_a=row2(b_a[0]), w_q=w_q, w_v=w_v,
                             qk=qk, vd=vd, dqk=dqk, tm=256)
    heads = dqk // (2 * DIFF_HEAD_DIM)

    xp = x_prompt.reshape(nbp * tp, d)
    gq, gk, gv, gg, la, dq, kt_p, dv_p, kb, vb = proj(xp, w_k=w_k.T, kt_frames=tp)
    dk_p = kt_p.reshape(1, nbp, heads, 2, DIFF_HEAD_DIM, tp).transpose(0, 1, 5, 2, 3, 4)
    og, state_p = _gla(gq, gk, gv, gg, la, gn, None, nb=nbp, t=tp, c=128, nsub=2)
    od = _attn_prompt(dq, kb, vb, lams, dg, nb=nbp, t=tp, tile=512, lam_init=lam_init)
    y_p, a_last = _ffn(xp, og, od, *ffn_w, tm=256, alpha=alpha, seq_len=tp)
    conv_p = a_last.reshape(nbp, -1, SUBLANES, d_ff)[:, -1, SUBLANES - (CONV_W - 1):, :]

    xs = x_sample.reshape(nbs * ts, d)
    gq, gk, gv, gg, la, dq, dk_s, dv_s, kb, vb = proj(xs, w_k=w_k)
    og, state_s = _gla(gq, gk, gv, gg, la, gn, state_gla[0], nb=nbs, t=ts, c=ts)
    k_cache_t = cache_diff_k[0].transpose(0, 2, 3, 4, 1).reshape(nbs, dqk, past_len)
    v_cache = cache_diff_v[0].reshape(nbs, past_len * heads, 2 * DIFF_HEAD_DIM)
    od = _attn_sample(dq, kb, vb, k_cache_t, v_cache, lams, dg, nb=nbs, t=ts, lam_init=lam_init)
    y_s, a_s = _ffn(xs, og, od, *ffn_w, tm=nbs * ts, alpha=alpha, seq_len=ts,
                    past=cache_ffn_conv[0].reshape(nbs * (CONV_W - 1), d_ff))
    conv_s = a_s.reshape(nbs, ts, d_ff)[:, ts - (CONV_W - 1):, :]

    v_shape = lambda nb, t: (1, nb, t, heads, 2 * DIFF_HEAD_DIM)
    return (y_p.reshape(nbp, tp, d), y_s.reshape(nbs, ts, d),
            dk_p, dv_p.reshape(v_shape(nbp, tp)), state_p[None], conv_p[None],
            dk_s.reshape(1, nbs, ts, heads, 2, DIFF_HEAD_DIM), dv_s.reshape(v_shape(nbs, ts)), state_s[None],
            conv_s[None])
```

```python
import functools
import math

import numpy as np
import jax
import jax.numpy as jnp
from jax import lax
from jax.experimental import pallas as pl
from jax.experimental.pallas import tpu as pltpu

F32 = jnp.float32
BF16 = jnp.bfloat16

CHUNK = 64
GLA_HEADS = 4
GLA_RANK = 16
GLA_TAU = 16.0
GLA_BLOCK = 16
DIFF_HEAD_DIM = 64
CONV_W = 3
EPS = 1e-5
LOG2_E = math.log2(math.e)
TILE_GROUPS = (4, 2, 1)

LANES = 128
SUBLANES = 8
VMEM_LIMIT = 56 * 1024 * 1024

NT_DIMS = (((1,), (1,)), ((), ()))
TN_DIMS = (((0,), (0,)), ((), ()))


def _sigmoid(x):
    return 1.0 / (1.0 + jnp.exp(-x))


def _layer_norm(x, g, b):
    mu = jnp.mean(x, -1, keepdims=True)
    xc = x - mu
    var = jnp.mean(xc * xc, -1, keepdims=True)
    return xc * lax.rsqrt(var + EPS) * g + b


def _resident(shape):
    nd = len(shape)
    return pl.BlockSpec(shape, lambda *_: (0,) * nd, pipeline_mode=pl.Buffered(1))


def _params(sem):
    return pltpu.CompilerParams(dimension_semantics=sem, vmem_limit_bytes=VMEM_LIMIT)


def _proj_kernel(x_ref, wgla_ref, wlow_ref, wa2_ref, ba_ref, wq_ref, wk_ref, wv_ref,
                 gq_ref, gk_ref, gv_ref, gg_ref, la_ref, dq_ref, dk_ref, dv_ref, kb_ref, vb_ref,
                 *, qk, vd, dqk, k_transposed):
    xb = x_ref[...].astype(BF16)

    def mm(w_ref, lo, hi):
        return jnp.dot(xb, w_ref[:, lo:hi], preferred_element_type=F32)

    gq_ref[...] = mm(wgla_ref, 0, qk)
    gk_ref[...] = mm(wgla_ref, qk, 2 * qk)
    gv_ref[...] = mm(wgla_ref, 2 * qk, 2 * qk + vd).astype(BF16)
    gg_ref[...] = mm(wgla_ref, 2 * qk + vd, 2 * qk + 2 * vd)

    g_low = jnp.dot(xb, wlow_ref[...], preferred_element_type=F32)
    z = jnp.dot(g_low.astype(BF16), wa2_ref[...], preferred_element_type=F32) + ba_ref[...]
    softplus_neg = jnp.maximum(-z, 0.0) + jnp.log1p(jnp.exp(-jnp.abs(z)))
    la_ref[...] = -softplus_neg * (1.0 / GLA_TAU)

    dq_ref[...] = (mm(wq_ref, 0, dqk) * (DIFF_HEAD_DIM ** -0.5 * LOG2_E)).astype(BF16)
    if k_transposed:
        dk = lax.dot_general(wk_ref[...], xb, NT_DIMS, preferred_element_type=F32)
        dk_ref[0] = dk
        kb_ref[0] = dk.astype(BF16)
    else:
        dk = mm(wk_ref, 0, dqk)
        dk_ref[...] = dk
        kb_ref[...] = dk.astype(BF16)
    dv = mm(wv_ref, 0, dqk)
    dv_ref[...] = dv
    vb_ref[...] = dv.astype(BF16)


def _proj(x, w_gla, w_low, w_a2, b_a, w_q, w_k, w_v, *, qk, vd, dqk, tm, kt_frames=None):
    rows, d = x.shape
    tm = min(tm, rows)
    row_spec = lambda n: pl.BlockSpec((tm, n), lambda i: (i, 0))
    out = lambda n, dt: jax.ShapeDtypeStruct((rows, n), dt)
    if kt_frames is None:
        k_spec, k_out = row_spec(dqk), lambda dt: out(dqk, dt)
    else:
        per_seq = kt_frames // tm
        k_spec = pl.BlockSpec((1, dqk, tm), lambda i: (i // per_seq, 0, i % per_seq))
        k_out = lambda dt: jax.ShapeDtypeStruct((rows // kt_frames, dqk, kt_frames), dt)
    weights = (w_gla, w_low, w_a2, b_a, w_q, w_k, w_v)
    return pl.pallas_call(
        functools.partial(_proj_kernel, qk=qk, vd=vd, dqk=dqk, k_transposed=kt_frames is not None),
        grid=(rows // tm,),
        in_specs=[row_spec(d)] + [_resident(a.shape) for a in weights],
        out_specs=[row_spec(qk), row_spec(qk), row_spec(vd), row_spec(vd), row_spec(qk),
                   row_spec(dqk), k_spec, row_spec(dqk), k_spec, row_spec(dqk)],
        out_shape=[out(qk, F32), out(qk, F32), out(vd, BF16), out(vd, F32), out(qk, F32),
                   out(dqk, BF16), k_out(F32), out(dqk, F32), k_out(BF16), out(dqk, BF16)],
        compiler_params=_params(("parallel",)),
        name="proj",
    )(x, *weights)


def _gla_levels(c):
    d = min(c, 2 * GLA_BLOCK)
    off = []
    b = c // 2
    while b >= d:
        off.append(b)
        b //= 2
    return off, d


def _row_broadcast_refs(cum, group, ref_offset):
    c, n = cum.shape
    parts = [jnp.broadcast_to(cum[p * group + ref_offset:p * group + ref_offset + 1, :], (group, n))
             for p in range(c // group)]
    return parts[0] if len(parts) == 1 else jnp.concatenate(parts, axis=0)


def _gla_kernel(*refs, c, nsub, nseq, dk, dv, has_s0):
    if has_s0:
        q_ref, k_ref, v_ref, gg_ref, la_ref, gn_ref, s0_ref, o_ref, sout_ref, s_scr = refs
    else:
        q_ref, k_ref, v_ref, gg_ref, la_ref, gn_ref, o_ref, sout_ref, s_scr = refs
    step = pl.program_id(1)

    @pl.when(step == 0)
    def _():
        s_scr[...] = s0_ref[...] if has_s0 else jnp.zeros_like(s_scr)

    for seq in range(nseq):
        for sub in range(nsub):
            first = (seq * nsub + sub) * c
            _gla_chunk(q_ref, k_ref, v_ref, gg_ref, la_ref, gn_ref, o_ref, s_scr.at[seq],
                       slice(first, first + c), c=c, dk=dk, dv=dv)

    @pl.when(step == pl.num_programs(1) - 1)
    def _():
        sout_ref[...] = s_scr[...]


def _gla_chunk(q_ref, k_ref, v_ref, gg_ref, la_ref, gn_ref, o_ref, s_scr, rows, *, c, dk, dv):
    row = lax.broadcasted_iota(jnp.int32, (c, c), 0)
    col = lax.broadcasted_iota(jnp.int32, (c, c), 1)
    tri = (col <= row).astype(F32)
    cum_all = jnp.dot(tri, la_ref[rows, :], precision=lax.Precision.HIGHEST, preferred_element_type=F32)

    off_levels, d = _gla_levels(c)
    scale = dk ** -0.5
    gn = gn_ref[...]

    for h in range(GLA_HEADS):
        ks = slice(h * dk, (h + 1) * dk)
        vs = slice(h * dv, (h + 1) * dv)
        cum = cum_all[:, ks]
        qs = q_ref[rows, ks] * scale
        kk = k_ref[rows, ks]
        vv = v_ref[rows, vs]

        ref = _row_broadcast_refs(cum, d, d // 2 - 1)
        qd = (qs * jnp.exp(cum - ref)).astype(BF16)
        kd = (kk * jnp.exp(ref - cum)).astype(BF16)
        att = lax.dot_general(qd, kd, NT_DIMS, preferred_element_type=F32)
        att = jnp.where((row // d == col // d) & (col <= row), att, 0.0)
        for b in off_levels:
            ref = _row_broadcast_refs(cum, 2 * b, b - 1)
            ql = (qs * jnp.exp(jnp.minimum(cum - ref, 0.0))).astype(BF16)
            kl = (kk * jnp.exp(jnp.minimum(ref - cum, 0.0))).astype(BF16)
            a_l = lax.dot_general(ql, kl, NT_DIMS, preferred_element_type=F32)
            att = jnp.where(((row // b) % 2 == 1) & (col // b == row // b - 1), a_l, att)

        s_h = s_scr[h]
        q_dec = (qs * jnp.exp(cum)).astype(BF16)
        o = (jnp.dot(att.astype(BF16), vv, preferred_element_type=F32)
             + jnp.dot(q_dec, s_h.astype(BF16), preferred_element_type=F32))

        last = cum[c - 1:c, :]
        k_end = (kk * jnp.exp(last - cum)).astype(BF16)
        kv = lax.dot_general(k_end, vv, TN_DIMS, preferred_element_type=F32)
        decay_col = jnp.transpose(jnp.broadcast_to(jnp.exp(last), (dk, dk)))
        decay = jnp.concatenate([decay_col] * (dv // dk), axis=1)
        s_scr[h] = decay * s_h + kv

        ms = jnp.mean(o * o, -1, keepdims=True)
        gate = gg_ref[rows, vs]
        o_ref[rows, vs] = o * lax.rsqrt(ms + EPS) * gn * (gate * _sigmoid(gate))


def _gla(gq, gk, gv, gg, la, gn, s0, *, nb, t, c, nsub=1, nseq=1):
    rows, qk = gq.shape
    vd = gv.shape[1]
    dk, dv = qk // GLA_HEADS, vd // GLA_HEADS
    c = min(c, t)
    nsub = min(nsub, t // c)
    nc = t // (c * nsub)
    assert nseq == 1 or (nc == 1 and nb % nseq == 0), "several sequences per step only when a step spans them whole"
    blk = lambda n: pl.BlockSpec((nseq * c * nsub, n), lambda b, s: (b * nc + s, 0))
    state_spec = pl.BlockSpec((nseq, GLA_HEADS, dk, dv), lambda b, s: (b, 0, 0, 0))
    in_specs = [blk(qk), blk(qk), blk(vd), blk(vd), blk(qk), _resident(gn.shape)]
    args = [gq, gk, gv, gg, la, gn]
    if s0 is not None:
        in_specs.append(state_spec)
        args.append(s0)
    return pl.pallas_call(
        functools.partial(_gla_kernel, c=c, nsub=nsub, nseq=nseq, dk=dk, dv=dv, has_s0=s0 is not None),
        grid=(nb // nseq, nc),
        in_specs=in_specs,
        out_specs=[blk(vd), state_spec],
        out_shape=[jax.ShapeDtypeStruct((rows, vd), F32),
                   jax.ShapeDtypeStruct((nb, GLA_HEADS, dk, dv), F32)],
        scratch_shapes=[pltpu.VMEM((nseq, GLA_HEADS, dk, dv), F32)],
        compiler_params=_params(("parallel", "arbitrary")),
        name="gla",
    )(*args)


def _stack_half_queries(q):
    lane = lax.broadcasted_iota(jnp.int32, q.shape, 1)
    zero = jnp.zeros_like(q)
    return jnp.concatenate([jnp.where(lane < DIFF_HEAD_DIM, q, zero),
                            jnp.where(lane >= DIFF_HEAD_DIM, q, zero)], axis=0)


def _lambda(lq1_ref, lk1_ref, lq2_ref, lk2_ref, lam_init):
    return (jnp.exp(jnp.sum(lq1_ref[...] * lk1_ref[...], -1, keepdims=True))
            - jnp.exp(jnp.sum(lq2_ref[...] * lk2_ref[...], -1, keepdims=True)) + lam_init)


def _diff_finish(acc, l, lam, g, lam_init, tq):
    o = acc * (1.0 / l)
    od = o[:tq] - lam * o[tq:]
    ms = jnp.mean(od * od, -1, keepdims=True)
    return od * lax.rsqrt(ms + EPS) * g * (1.0 - lam_init)


def _attn_prompt_kernel(q_ref, k_ref, v_ref, lq1_ref, lk1_ref, lq2_ref, lk2_ref, g_ref, o_ref,
                        qs_scr, s_scr, p_scr, m_scr, l_scr, a_scr, acc_scr, *, tq, tk, lam_init):
    i = pl.program_id(2)
    q_first = i * tq
    qs_scr[...] = _stack_half_queries(q_ref[...])
    m_scr[...] = jnp.full(m_scr.shape, -jnp.inf, F32)
    l_scr[...] = jnp.zeros(l_scr.shape, F32)
    acc_scr[...] = jnp.zeros(acc_scr.shape, F32)
    reps = tk // LANES

    def tile(j, carry, diagonal):
        start = pl.multiple_of(j * tk, tk)
        s_scr[...] = jnp.dot(qs_scr[...], k_ref[:, pl.ds(start, tk)],
                             preferred_element_type=F32)
        for r0 in range(0, 2 * tq, CHUNK):
            rows = slice(r0, r0 + CHUNK)
            limit = ((r0 % tq) // CHUNK + 1) * CHUNK if diagonal else tk
            pieces = []
            for g in range(reps):
                lo = g * LANES
                if lo >= limit:
                    pieces.append(None)
                elif lo + LANES <= limit:
                    pieces.append(s_scr[rows, lo:lo + LANES])
                else:
                    lane = lax.broadcasted_iota(jnp.int32, (CHUNK, LANES), 1)
                    pieces.append(jnp.where(lane < limit - lo, s_scr[rows, lo:lo + LANES], -jnp.inf))
            seen = [x for x in pieces if x is not None]
            m_prev = m_scr[rows, :]
            m_next = jnp.maximum(m_prev, jnp.max(functools.reduce(jnp.maximum, seen), -1, keepdims=True))
            alpha = jnp.exp2(m_prev - m_next)
            p_lanes = None
            for g, x in enumerate(pieces):
                lo = g * LANES
                if x is None:
                    p_scr[rows, lo:lo + LANES] = jnp.zeros((CHUNK, LANES), BF16)
                    continue
                p = jnp.exp2(x - m_next)
                p_lanes = p if p_lanes is None else p_lanes + p
                p_scr[rows, lo:lo + LANES] = p.astype(BF16)
            l_scr[rows, :] = alpha * l_scr[rows, :] + p_lanes
            m_scr[rows, :] = m_next
            a_scr[rows, :] = alpha
        acc_scr[...] = a_scr[...] * acc_scr[...] + jnp.dot(p_scr[...], v_ref[pl.ds(start, tk), :],
                                                           preferred_element_type=F32)
        return carry

    n_full = q_first // tk
    def tile_group(jj, carry, first, size):
        for u in range(size):
            tile(first + size * jj + u, carry, diagonal=False)
        return carry

    done = 0
    for size in TILE_GROUPS:
        trips = (n_full - done) // size
        lax.fori_loop(0, trips, functools.partial(tile_group, first=done, size=size), 0)
        done = done + trips * size
    tile(n_full, 0, diagonal=True)

    lam = _lambda(lq1_ref, lk1_ref, lq2_ref, lk2_ref, lam_init)
    l = jnp.sum(l_scr[...], -1, keepdims=True)
    o_ref[...] = _diff_finish(acc_scr[...], l, lam, g_ref[...], lam_init, tq)


def _attn_prompt(dq, kb, vb, lams, g, *, nb, t, tile, lam_init):
    rows, dqk = dq.shape
    hw = 2 * DIFF_HEAD_DIM
    heads = dqk // hw
    tq = tk = min(tile, t)
    assert tq % CHUNK == 0 and t % tq == 0
    nq = t // tq
    k_spec = pl.BlockSpec((None, hw, t), lambda b, h, i: (b, h, 0))
    v_spec = pl.BlockSpec((t, hw), lambda b, h, i: (b, h))
    q_spec = pl.BlockSpec((tq, hw), lambda b, h, i: (b * nq + i, h))
    small = [_resident(a.shape) for a in (*lams, g)]
    stat = pltpu.VMEM((2 * tq, LANES), F32)
    return pl.pallas_call(
        functools.partial(_attn_prompt_kernel, tq=tq, tk=tk, lam_init=lam_init),
        grid=(nb, heads, nq),
        in_specs=[q_spec, k_spec, v_spec, *small],
        out_specs=q_spec,
        out_shape=jax.ShapeDtypeStruct((rows, dqk), F32),
        scratch_shapes=[pltpu.VMEM((2 * tq, hw), BF16), pltpu.VMEM((2 * tq, tk), F32),
                        pltpu.VMEM((2 * tq, tk), BF16), stat, stat, stat, pltpu.VMEM((2 * tq, hw), F32)],
        compiler_params=_params(("parallel", "parallel", "arbitrary")),
        name="attn_prompt",
    )(dq, kb, vb, *lams, g)


def _attn_sample_kernel(q_ref, kn_ref, vn_ref, kc_ref, vc_ref, lq1_ref, lk1_ref, lq2_ref, lk2_ref, g_ref,
                        o_ref, *, heads, lam_init):
    tq = q_ref.shape[0]
    hw = 2 * DIFF_HEAD_DIM
    past = kc_ref.shape[2]
    lam = _lambda(lq1_ref, lk1_ref, lq2_ref, lk2_ref, lam_init)
    g = g_ref[...]
    for h in range(heads):
        hs = slice(h * hw, (h + 1) * hw)
        qs = _stack_half_queries(q_ref[:, hs])
        k_past = kc_ref[0, hs, :].astype(BF16)
        v_past = vc_ref[0, pl.ds(h, past, stride=heads), :].astype(BF16)
        s_past = jnp.dot(qs, k_past, preferred_element_type=F32)
        s_new = lax.dot_general(qs, kn_ref[:, hs], NT_DIMS, preferred_element_type=F32)
        m = jnp.maximum(jnp.max(s_past, -1, keepdims=True), jnp.max(s_new, -1, keepdims=True))
        p_past = jnp.exp2(s_past - m)
        p_new = jnp.exp2(s_new - m)
        l = jnp.sum(p_past, -1, keepdims=True) + jnp.sum(p_new, -1, keepdims=True)
        acc = (jnp.dot(p_past.astype(BF16), v_past, preferred_element_type=F32)
               + jnp.dot(p_new.astype(BF16), vn_ref[:, hs], preferred_element_type=F32))
        o_ref[:, hs] = _diff_finish(acc, l, lam, g, lam_init, tq)


def _attn_sample(dq, kb, vb, k_cache, v_cache, lams, g, *, nb, t, lam_init):
    rows, dqk = dq.shape
    hw = 2 * DIFF_HEAD_DIM
    heads = dqk // hw
    past = k_cache.shape[2]
    q_pos = past + np.arange(t)
    assert ((q_pos // CHUNK + 1) * CHUNK >= past + t).all(), "sample queries must see every cached and new key"
    new_spec = pl.BlockSpec((t, dqk), lambda b: (b, 0))
    k_cache_spec = pl.BlockSpec((1, dqk, past), lambda b: (b, 0, 0))
    v_cache_spec = pl.BlockSpec((1, past * heads, hw), lambda b: (b, 0, 0))
    small = [_resident(a.shape) for a in (*lams, g)]
    return pl.pallas_call(
        functools.partial(_attn_sample_kernel, heads=heads, lam_init=lam_init),
        grid=(nb,),
        in_specs=[new_spec, new_spec, new_spec, k_cache_spec, v_cache_spec, *small],
        out_specs=new_spec,
        out_shape=jax.ShapeDtypeStruct((rows, dqk), F32),
        compiler_params=_params(("parallel",)),
        name="attn_sample",
    )(dq, kb, vb, k_cache, v_cache, *lams, g)


def _gelu_tanh(x):
    return x * (0.5 * (1.0 + jnp.tanh(math.sqrt(2.0 / math.pi) * (x + 0.044715 * (x * x * x)))))


def _ffn_kernel(*refs, tm, d_ff, fc, alpha, blocks_per_seq, seg):
    sample = seg is not None
    if sample:
        (x_ref, og_ref, od_ref, wg_ref, wo_ref, l1g_ref, l1b_ref, wup_ref, cw_ref, cb_ref, wdn_ref,
         l2g_ref, l2b_ref, past_ref, sel1_ref, sel2_ref, y_ref, a_ref,
         a_scr, x1_scr, x1b_scr, g_scr, h_scr, acc_scr) = refs
    else:
        (x_ref, og_ref, od_ref, wg_ref, wo_ref, l1g_ref, l1b_ref, wup_ref, cw_ref, cb_ref, wdn_ref,
         l2g_ref, l2b_ref, y_ref, a_ref,
         a_scr, x1_scr, x1b_scr, g_scr, h_scr, acc_scr) = refs
    d = x_ref.shape[1]
    pad = SUBLANES
    if sample:
        a_scr[0:pad, :] = jnp.zeros((pad, d_ff), F32)
        r = lax.broadcasted_iota(jnp.int32, (tm, fc), 0) % seg
    else:
        first = pl.program_id(0) % blocks_per_seq == 0

        @pl.when(first)
        def _():
            a_scr[0:pad, :] = jnp.zeros((pad, d_ff), F32)

        @pl.when(jnp.logical_not(first))
        def _():
            a_scr[0:pad, :] = a_scr[tm:tm + pad, :]

    xb_scr, mix_scr = x1b_scr, acc_scr
    xb_scr[...] = x_ref[...].astype(BF16)
    gate_a = jnp.dot(xb_scr[...], wg_ref[:, 0:d], preferred_element_type=F32)
    mix_scr[...] = _sigmoid(gate_a) * og_ref[...]
    gate_b = jnp.dot(xb_scr[...], wg_ref[:, d:2 * d], preferred_element_type=F32)
    xb_scr[...] = (mix_scr[...] + _sigmoid(gate_b) * od_ref[...]).astype(BF16)
    x1 = _layer_norm(alpha * x_ref[...] + jnp.dot(xb_scr[...], wo_ref[...], preferred_element_type=F32),
                     l1g_ref[...], l1b_ref[...])
    x1_scr[...] = x1
    x1b_scr[...] = x1.astype(BF16)

    def up(c):
        a_scr[pad:pad + tm, c * fc:(c + 1) * fc] = jnp.dot(
            x1b_scr[...], wup_ref[:, c * fc:(c + 1) * fc], preferred_element_type=F32)
        g_scr[c % 2] = jnp.dot(x1b_scr[...], wup_ref[:, d_ff + c * fc:d_ff + (c + 1) * fc],
                               preferred_element_type=F32)

    n_chunks = d_ff // fc
    up(0)
    for c in range(n_chunks):
        cs = slice(c * fc, (c + 1) * fc)
        if c + 1 < n_chunks:
            up(c + 1)
        a = a_scr[pad:pad + tm, cs]
        gate = g_scr[c % 2]
        a1 = a_scr[pad - 1:pad - 1 + tm, cs]
        a2 = a_scr[pad - 2:pad - 2 + tm, cs]
        if sample:
            past = past_ref[:, cs]
            hist1 = jnp.dot(sel1_ref[...], past, precision=lax.Precision.HIGHEST, preferred_element_type=F32)
            hist2 = jnp.dot(sel2_ref[...], past, precision=lax.Precision.HIGHEST, preferred_element_type=F32)
            a1 = jnp.where(r == 0, 0.0, a1) + hist1
            a2 = jnp.where(r < 2, 0.0, a2) + hist2
        conv = cb_ref[:, cs] + a2 * cw_ref[0:1, cs] + a1 * cw_ref[1:2, cs] + a * cw_ref[2:3, cs]
        h_scr[c % 2] = (_gelu_tanh(conv) * gate).astype(BF16)
        down = jnp.dot(h_scr[c % 2], wdn_ref[cs, :], preferred_element_type=F32)
        if c == 0:
            acc_scr[...] = down
        else:
            acc_scr[...] += down

    if sample:
        a_ref[...] = a_scr[pad:pad + tm, :]
    else:
        a_ref[0] = a_scr[tm:tm + pad, :]
    y_ref[...] = _layer_norm(alpha * x1_scr[...] + acc_scr[...], l2g_ref[...], l2b_ref[...])


def _ffn(x, og, od, w_gate, w_out, l1g, l1b, w_up, conv_w, conv_b, w_down, l2g, l2b,
         *, tm, alpha, seq_len, past=None, fc=256):
    rows, d = x.shape
    d_ff = w_down.shape[0]
    tm = min(tm, rows)
    nblk = rows // tm
    sample = past is not None
    row_spec = pl.BlockSpec((tm, d), lambda i: (i, 0))
    weights = [w_gate, w_out, l1g, l1b, w_up, conv_w, conv_b, w_down, l2g, l2b]
    args = [x, og, od, *weights]
    scratch = [pltpu.VMEM((tm + SUBLANES, d_ff), F32), pltpu.VMEM((tm, d), F32), pltpu.VMEM((tm, d), BF16),
               pltpu.VMEM((2, tm, fc), F32), pltpu.VMEM((2, tm, fc), BF16), pltpu.VMEM((tm, d), F32)]
    if sample:
        nseq = tm // seq_len
        sel1 = np.zeros((tm, past.shape[0]), np.float32)
        sel2 = np.zeros((tm, past.shape[0]), np.float32)
        for s in range(nseq):
            sel1[s * seq_len, 2 * s + 1] = 1.0
            sel2[s * seq_len, 2 * s] = 1.0
            sel2[s * seq_len + 1, 2 * s + 1] = 1.0
        assert nblk == 1
        args += [past, jnp.asarray(sel1), jnp.asarray(sel2)]
        a_shape = jax.ShapeDtypeStruct((rows, d_ff), F32)
        a_spec = pl.BlockSpec((tm, d_ff), lambda i: (i, 0))
        kern = functools.partial(_ffn_kernel, tm=tm, d_ff=d_ff, fc=fc, alpha=alpha, blocks_per_seq=None, seg=seq_len)
    else:
        a_shape = jax.ShapeDtypeStruct((nblk, SUBLANES, d_ff), F32)
        a_spec = pl.BlockSpec((1, SUBLANES, d_ff), lambda i: (i, 0, 0))
        kern = functools.partial(_ffn_kernel, tm=tm, d_ff=d_ff, fc=fc, alpha=alpha,
                                 blocks_per_seq=seq_len // tm, seg=None)
    in_specs = [row_spec, row_spec, row_spec] + [_resident(a.shape) for a in args[3:]]
    return pl.pallas_call(
        kern,
        grid=(nblk,),
        in_specs=in_specs,
        out_specs=[row_spec, a_spec],
        out_shape=[jax.ShapeDtypeStruct((rows, d), F32), a_shape],
        scratch_shapes=scratch,
        compiler_params=_params(("arbitrary",)),
        name="ffn",
    )(*args)


def kernel(x_prompt, x_sample, cache_diff_k, cache_diff_v, state_gla, cache_ffn_conv, w_in, w_a2, b_a, gla_norm_g, lam_q1, lam_k1, lam_q2, lam_k2, diff_norm_g, w_out, ln1_g, ln1_b, w_up, conv_w, conv_b, w_down, ln2_g, ln2_b):
    depth = w_in.shape[0]
    assert depth == 1, "single-layer step"
    nbp, tp, d = x_prompt.shape
    nbs, ts, _ = x_sample.shape
    past_len = cache_diff_k.shape[2]
    qk = w_a2.shape[2]
    vd = GLA_HEADS * gla_norm_g.shape[1]
    dqk = cache_diff_v.shape[3] * cache_diff_v.shape[4]
    d_ff = w_down.shape[1]
    alpha = (2.0 * depth) ** 0.25
    lam_init = 0.8 - 0.6 * math.exp(-0.3 * 0)

    w = w_in[0]
    o_low = 2 * qk + 2 * vd
    o_diff = o_low + GLA_RANK
    o_gate = o_diff + 3 * dqk
    w_gla = w[:, :o_low].astype(BF16)
    w_low = jnp.pad(w[:, o_low:o_diff], ((0, 0), (0, LANES - GLA_RANK))).astype(BF16)
    w_q = w[:, o_diff:o_diff + dqk].astype(BF16)
    w_k = w[:, o_diff + dqk:o_diff + 2 * dqk].astype(BF16)
    w_v = w[:, o_diff + 2 * dqk:o_gate].astype(BF16)
    w_gate = w[:, o_gate:].astype(BF16)
    w_a2p = jnp.pad(w_a2[0], ((0, LANES - GLA_RANK), (0, 0))).astype(BF16)
    row2 = lambda a: a.reshape(1, -1)
    lams = [row2(lam_q1[0]), row2(lam_k1[0]), row2(lam_q2[0]), row2(lam_k2[0])]
    ffn_w = (w_gate, w_out[0].astype(BF16), row2(ln1_g[0]), row2(ln1_b[0]), w_up[0].astype(BF16), conv_w[0],
             row2(conv_b[0]), w_down[0].astype(BF16), row2(ln2_g[0]), row2(ln2_b[0]))
    gn = row2(gla_norm_g[0])
    dg = row2(diff_norm_g[0])
    proj = functools.partial(_proj, w_gla=w_gla, w_low=w_low, w_a2=w_a2p, b_a=row2(b_a[0]), w_q=w_q, w_v=w_v,
                             qk=qk, vd=vd, dqk=dqk, tm=512)
    heads = dqk // (2 * DIFF_HEAD_DIM)

    xp = x_prompt.reshape(nbp * tp, d)
    gq, gk, gv, gg, la, dq, kt_p, dv_p, kb, vb = proj(xp, w_k=w_k.T, kt_frames=tp)
    dk_p = kt_p.reshape(1, nbp, heads, 2, DIFF_HEAD_DIM, tp).transpose(0, 1, 5, 2, 3, 4)
    og, state_p = _gla(gq, gk, gv, gg, la, gn, None, nb=nbp, t=tp, c=128, nsub=4)
    od = _attn_prompt(dq, kb, vb, lams, dg, nb=nbp, t=tp, tile=512, lam_init=lam_init)
    y_p, a_last = _ffn(xp, og, od, *ffn_w, tm=256, alpha=alpha, seq_len=tp)
    conv_p = a_last.reshape(nbp, -1, SUBLANES, d_ff)[:, -1, SUBLANES - (CONV_W - 1):, :]

    xs = x_sample.reshape(nbs * ts, d)
    gq, gk, gv, gg, la, dq, dk_s, dv_s, kb, vb = proj(xs, w_k=w_k)
    og, state_s = _gla(gq, gk, gv, gg, la, gn, state_gla[0], nb=nbs, t=ts, c=ts, nseq=math.gcd(nbs, 4))
    k_cache_t = cache_diff_k[0].transpose(0, 2, 3, 4, 1).reshape(nbs, dqk, past_len)
    v_cache = cache_diff_v[0].reshape(nbs, past_len * heads, 2 * DIFF_HEAD_DIM)
    od = _attn_sample(dq, kb, vb, k_cache_t, v_cache, lams, dg, nb=nbs, t=ts, lam_init=lam_init)
    y_s, a_s = _ffn(xs, og, od, *ffn_w, tm=nbs * ts, alpha=alpha, seq_len=ts,
                    past=cache_ffn_conv[0].reshape(nbs * (CONV_W - 1), d_ff))
    conv_s = a_s.reshape(nbs, ts, d_ff)[:, ts - (CONV_W - 1):, :]

    v_shape = lambda nb, t: (1, nb, t, heads, 2 * DIFF_HEAD_DIM)
    return (y_p.reshape(nbp, tp, d), y_s.reshape(nbs, ts, d),
            dk_p, dv_p.reshape(v_shape(nbp, tp)), state_p[None], conv_p[None],
            dk_s.reshape(1, nbs, ts, heads, 2, DIFF_HEAD_DIM), dv_s.reshape(v_shape(nbs, ts)), state_s[None],
            conv_s[None])
```

```python
import functools
import math

import numpy as np
import jax
import jax.numpy as jnp
from jax import lax
from jax.experimental import pallas as pl
from jax.experimental.pallas import tpu as pltpu

F32 = jnp.float32
BF16 = jnp.bfloat16

CHUNK = 64
GLA_HEADS = 4
GLA_RANK = 16
GLA_TAU = 16.0
GLA_BLOCK = 16
DIFF_HEAD_DIM = 64
CONV_W = 3
EPS = 1e-5
LOG2_E = math.log2(math.e)
TILE_GROUPS = (4, 2, 1)

LANES = 128
SUBLANES = 8
VMEM_LIMIT = 56 * 1024 * 1024

NT_DIMS = (((1,), (1,)), ((), ()))
TN_DIMS = (((0,), (0,)), ((), ()))


def _sigmoid(x):
    return 1.0 / (1.0 + jnp.exp(-x))


def _layer_norm(x, g, b):
    mu = jnp.mean(x, -1, keepdims=True)
    xc = x - mu
    var = jnp.mean(xc * xc, -1, keepdims=True)
    return xc * lax.rsqrt(var + EPS) * g + b


def _resident(shape):
    nd = len(shape)
    return pl.BlockSpec(shape, lambda *_: (0,) * nd, pipeline_mode=pl.Buffered(1))


def _params(sem):
    return pltpu.CompilerParams(dimension_semantics=sem, vmem_limit_bytes=VMEM_LIMIT)


def _proj_kernel(x_ref, wgla_ref, wlow_ref, wa2_ref, ba_ref, wq_ref, wk_ref, wv_ref,
                 gq_ref, gk_ref, gv_ref, gg_ref, la_ref, dq_ref, dk_ref, dv_ref, kb_ref, vb_ref,
                 *, qk, vd, dqk, k_transposed):
    xb = x_ref[...].astype(BF16)

    def mm(w_ref, lo, hi):
        return jnp.dot(xb, w_ref[:, lo:hi], preferred_element_type=F32)

    gq_ref[...] = mm(wgla_ref, 0, qk)
    gk_ref[...] = mm(wgla_ref, qk, 2 * qk)
    gv_ref[...] = mm(wgla_ref, 2 * qk, 2 * qk + vd).astype(BF16)
    gg_ref[...] = mm(wgla_ref, 2 * qk + vd, 2 * qk + 2 * vd)

    g_low = jnp.dot(xb, wlow_ref[...], preferred_element_type=F32)
    z = jnp.dot(g_low.astype(BF16), wa2_ref[...], preferred_element_type=F32) + ba_ref[...]
    softplus_neg = jnp.maximum(-z, 0.0) + jnp.log1p(jnp.exp(-jnp.abs(z)))
    la_ref[...] = -softplus_neg * (LOG2_E / GLA_TAU)

    dq_ref[...] = (mm(wq_ref, 0, dqk) * (DIFF_HEAD_DIM ** -0.5 * LOG2_E)).astype(BF16)
    if k_transposed:
        dk = lax.dot_general(wk_ref[...], xb, NT_DIMS, preferred_element_type=F32)
        dk_ref[0] = dk
        kb_ref[0] = dk.astype(BF16)
    else:
        dk = mm(wk_ref, 0, dqk)
        dk_ref[...] = dk
        kb_ref[...] = dk.astype(BF16)
    dv = mm(wv_ref, 0, dqk)
    dv_ref[...] = dv
    vb_ref[...] = dv.astype(BF16)


def _proj(x, w_gla, w_low, w_a2, b_a, w_q, w_k, w_v, *, qk, vd, dqk, tm, kt_frames=None):
    rows, d = x.shape
    tm = min(tm, rows)
    row_spec = lambda n: pl.BlockSpec((tm, n), lambda i: (i, 0))
    out = lambda n, dt: jax.ShapeDtypeStruct((rows, n), dt)
    if kt_frames is None:
        k_spec, k_out = row_spec(dqk), lambda dt: out(dqk, dt)
    else:
        per_seq = kt_frames // tm
        k_spec = pl.BlockSpec((1, dqk, tm), lambda i: (i // per_seq, 0, i % per_seq))
        k_out = lambda dt: jax.ShapeDtypeStruct((rows // kt_frames, dqk, kt_frames), dt)
    weights = (w_gla, w_low, w_a2, b_a, w_q, w_k, w_v)
    return pl.pallas_call(
        functools.partial(_proj_kernel, qk=qk, vd=vd, dqk=dqk, k_transposed=kt_frames is not None),
        grid=(rows // tm,),
        in_specs=[row_spec(d)] + [_resident(a.shape) for a in weights],
        out_specs=[row_spec(qk), row_spec(qk), row_spec(vd), row_spec(vd), row_spec(qk),
                   row_spec(dqk), k_spec, row_spec(dqk), k_spec, row_spec(dqk)],
        out_shape=[out(qk, F32), out(qk, F32), out(vd, BF16), out(vd, F32), out(qk, F32),
                   out(dqk, BF16), k_out(F32), out(dqk, F32), k_out(BF16), out(dqk, BF16)],
        compiler_params=_params(("parallel",)),
        name="proj",
    )(x, *weights)


def _gla_levels(c):
    d = min(c, 2 * GLA_BLOCK)
    off = []
    b = c // 2
    while b >= d:
        off.append(b)
        b //= 2
    return off, d


def _row_broadcast_refs(cum, group, ref_offset):
    c, n = cum.shape
    parts = [jnp.broadcast_to(cum[p * group + ref_offset:p * group + ref_offset + 1, :], (group, n))
             for p in range(c // group)]
    return parts[0] if len(parts) == 1 else jnp.concatenate(parts, axis=0)


def _gla_kernel(*refs, c, nsub, nseq, dk, dv, has_s0):
    if has_s0:
        q_ref, k_ref, v_ref, gg_ref, la_ref, gn_ref, s0_ref, o_ref, sout_ref, s_scr = refs
    else:
        q_ref, k_ref, v_ref, gg_ref, la_ref, gn_ref, o_ref, sout_ref, s_scr = refs
    step = pl.program_id(1)

    @pl.when(step == 0)
    def _():
        s_scr[...] = s0_ref[...] if has_s0 else jnp.zeros_like(s_scr)

    for seq in range(nseq):
        for sub in range(nsub):
            first = (seq * nsub + sub) * c
            _gla_chunk(q_ref, k_ref, v_ref, gg_ref, la_ref, gn_ref, o_ref, s_scr.at[seq],
                       slice(first, first + c), c=c, dk=dk, dv=dv)

    @pl.when(step == pl.num_programs(1) - 1)
    def _():
        sout_ref[...] = s_scr[...]


def _gla_chunk(q_ref, k_ref, v_ref, gg_ref, la_ref, gn_ref, o_ref, s_scr, rows, *, c, dk, dv):
    row = lax.broadcasted_iota(jnp.int32, (c, c), 0)
    col = lax.broadcasted_iota(jnp.int32, (c, c), 1)
    tri = (col <= row).astype(F32)
    cum_all = jnp.dot(tri, la_ref[rows, :], precision=lax.Precision.HIGHEST, preferred_element_type=F32)

    off_levels, d = _gla_levels(c)
    scale = dk ** -0.5
    gn = gn_ref[...]

    for h in range(GLA_HEADS):
        ks = slice(h * dk, (h + 1) * dk)
        vs = slice(h * dv, (h + 1) * dv)
        cum = cum_all[:, ks]
        qs = q_ref[rows, ks] * scale
        kk = k_ref[rows, ks]
        vv = v_ref[rows, vs]

        ref = _row_broadcast_refs(cum, d, d // 2 - 1)
        qd = (qs * jnp.exp2(cum - ref)).astype(BF16)
        kd = (kk * jnp.exp2(ref - cum)).astype(BF16)
        att = lax.dot_general(qd, kd, NT_DIMS, preferred_element_type=F32)
        att = jnp.where((row // d == col // d) & (col <= row), att, 0.0)
        for b in off_levels:
            ref = _row_broadcast_refs(cum, 2 * b, b - 1)
            ql = (qs * jnp.exp2(jnp.minimum(cum - ref, 0.0))).astype(BF16)
            kl = (kk * jnp.exp2(jnp.minimum(ref - cum, 0.0))).astype(BF16)
            a_l = lax.dot_general(ql, kl, NT_DIMS, preferred_element_type=F32)
            att = jnp.where(((row // b) % 2 == 1) & (col // b == row // b - 1), a_l, att)

        s_h = s_scr[h]
        q_dec = (qs * jnp.exp2(cum)).astype(BF16)
        o = (jnp.dot(att.astype(BF16), vv, preferred_element_type=F32)
             + jnp.dot(q_dec, s_h.astype(BF16), preferred_element_type=F32))

        last = cum[c - 1:c, :]
        k_end = (kk * jnp.exp2(last - cum)).astype(BF16)
        kv = lax.dot_general(k_end, vv, TN_DIMS, preferred_element_type=F32)
        decay_col = jnp.transpose(jnp.broadcast_to(jnp.exp2(last), (dk, dk)))
        decay = jnp.concatenate([decay_col] * (dv // dk), axis=1)
        s_scr[h] = decay * s_h + kv

        ms = jnp.mean(o * o, -1, keepdims=True)
        gate = gg_ref[rows, vs]
        o_ref[rows, vs] = o * lax.rsqrt(ms + EPS) * gn * (gate * _sigmoid(gate))


def _gla(gq, gk, gv, gg, la, gn, s0, *, nb, t, c, nsub=1, nseq=1):
    rows, qk = gq.shape
    vd = gv.shape[1]
    dk, dv = qk // GLA_HEADS, vd // GLA_HEADS
    c = min(c, t)
    nsub = min(nsub, t // c)
    nc = t // (c * nsub)
    assert nseq == 1 or (nc == 1 and nb % nseq == 0), "several sequences per step only when a step spans them whole"
    blk = lambda n: pl.BlockSpec((nseq * c * nsub, n), lambda b, s: (b * nc + s, 0))
    state_spec = pl.BlockSpec((nseq, GLA_HEADS, dk, dv), lambda b, s: (b, 0, 0, 0))
    in_specs = [blk(qk), blk(qk), blk(vd), blk(vd), blk(qk), _resident(gn.shape)]
    args = [gq, gk, gv, gg, la, gn]
    if s0 is not None:
        in_specs.append(state_spec)
        args.append(s0)
    return pl.pallas_call(
        functools.partial(_gla_kernel, c=c, nsub=nsub, nseq=nseq, dk=dk, dv=dv, has_s0=s0 is not None),
        grid=(nb // nseq, nc),
        in_specs=in_specs,
        out_specs=[blk(vd), state_spec],
        out_shape=[jax.ShapeDtypeStruct((rows, vd), F32),
                   jax.ShapeDtypeStruct((nb, GLA_HEADS, dk, dv), F32)],
        scratch_shapes=[pltpu.VMEM((nseq, GLA_HEADS, dk, dv), F32)],
        compiler_params=_params(("parallel", "arbitrary")),
        name="gla",
    )(*args)


def _stack_half_queries(q):
    lane = lax.broadcasted_iota(jnp.int32, q.shape, 1)
    zero = jnp.zeros_like(q)
    return jnp.concatenate([jnp.where(lane < DIFF_HEAD_DIM, q, zero),
                            jnp.where(lane >= DIFF_HEAD_DIM, q, zero)], axis=0)


def _lambda(lq1_ref, lk1_ref, lq2_ref, lk2_ref, lam_init):
    return (jnp.exp(jnp.sum(lq1_ref[...] * lk1_ref[...], -1, keepdims=True))
            - jnp.exp(jnp.sum(lq2_ref[...] * lk2_ref[...], -1, keepdims=True)) + lam_init)


def _diff_finish(acc, l, lam, g, lam_init, tq):
    o = acc * (1.0 / l)
    od = o[:tq] - lam * o[tq:]
    ms = jnp.mean(od * od, -1, keepdims=True)
    return od * lax.rsqrt(ms + EPS) * g * (1.0 - lam_init)


def _attn_prompt_kernel(q_ref, k_ref, v_ref, lq1_ref, lk1_ref, lq2_ref, lk2_ref, g_ref, o_ref,
                        qs_scr, s_scr, p_scr, m_scr, l_scr, a_scr, acc_scr, *, tq, tk, lam_init):
    i = pl.program_id(2)
    q_first = i * tq
    qs_scr[...] = _stack_half_queries(q_ref[...])
    m_scr[...] = jnp.full(m_scr.shape, -jnp.inf, F32)
    l_scr[...] = jnp.zeros(l_scr.shape, F32)
    acc_scr[...] = jnp.zeros(acc_scr.shape, F32)
    reps = tk // LANES

    def tile(j, carry, diagonal):
        start = pl.multiple_of(j * tk, tk)
        s_scr[...] = jnp.dot(qs_scr[...], k_ref[:, pl.ds(start, tk)],
                             preferred_element_type=F32)
        for r0 in range(0, 2 * tq, CHUNK):
            rows = slice(r0, r0 + CHUNK)
            limit = ((r0 % tq) // CHUNK + 1) * CHUNK if diagonal else tk
            pieces = []
            for g in range(reps):
                lo = g * LANES
                if lo >= limit:
                    pieces.append(None)
                elif lo + LANES <= limit:
                    pieces.append(s_scr[rows, lo:lo + LANES])
                else:
                    lane = lax.broadcasted_iota(jnp.int32, (CHUNK, LANES), 1)
                    pieces.append(jnp.where(lane < limit - lo, s_scr[rows, lo:lo + LANES], -jnp.inf))
            seen = [x for x in pieces if x is not None]
            m_prev = m_scr[rows, :]
            m_next = jnp.maximum(m_prev, jnp.max(functools.reduce(jnp.maximum, seen), -1, keepdims=True))
            alpha = jnp.exp2(m_prev - m_next)
            p_lanes = None
            for g, x in enumerate(pieces):
                lo = g * LANES
                if x is None:
                    p_scr[rows, lo:lo + LANES] = jnp.zeros((CHUNK, LANES), BF16)
                    continue
                p = jnp.exp2(x - m_next)
                p_lanes = p if p_lanes is None else p_lanes + p
                p_scr[rows, lo:lo + LANES] = p.astype(BF16)
            l_scr[rows, :] = alpha * l_scr[rows, :] + p_lanes
            m_scr[rows, :] = m_next
            a_scr[rows, :] = alpha
        acc_scr[...] = a_scr[...] * acc_scr[...] + jnp.dot(p_scr[...], v_ref[pl.ds(start, tk), :],
                                                           preferred_element_type=F32)
        return carry

    n_full = q_first // tk
    def tile_group(jj, carry, first, size):
        for u in range(size):
            tile(first + size * jj + u, carry, diagonal=False)
        return carry

    done = 0
    for size in TILE_GROUPS:
        trips = (n_full - done) // size
        lax.fori_loop(0, trips, functools.partial(tile_group, first=done, size=size), 0)
        done = done + trips * size
    tile(n_full, 0, diagonal=True)

    lam = _lambda(lq1_ref, lk1_ref, lq2_ref, lk2_ref, lam_init)
    l = jnp.sum(l_scr[...], -1, keepdims=True)
    o_ref[...] = _diff_finish(acc_scr[...], l, lam, g_ref[...], lam_init, tq)


def _attn_prompt(dq, kb, vb, lams, g, *, nb, t, tile, lam_init):
    rows, dqk = dq.shape
    hw = 2 * DIFF_HEAD_DIM
    heads = dqk // hw
    tq = tk = min(tile, t)
    assert tq % CHUNK == 0 and t % tq == 0
    nq = t // tq
    k_spec = pl.BlockSpec((None, hw, t), lambda b, h, i: (b, h, 0))
    v_spec = pl.BlockSpec((t, hw), lambda b, h, i: (b, h))
    q_spec = pl.BlockSpec((tq, hw), lambda b, h, i: (b * nq + i, h))
    small = [_resident(a.shape) for a in (*lams, g)]
    stat = pltpu.VMEM((2 * tq, LANES), F32)
    return pl.pallas_call(
        functools.partial(_attn_prompt_kernel, tq=tq, tk=tk, lam_init=lam_init),
        grid=(nb, heads, nq),
        in_specs=[q_spec, k_spec, v_spec, *small],
        out_specs=q_spec,
        out_shape=jax.ShapeDtypeStruct((rows, dqk), F32),
        scratch_shapes=[pltpu.VMEM((2 * tq, hw), BF16), pltpu.VMEM((2 * tq, tk), F32),
                        pltpu.VMEM((2 * tq, tk), BF16), stat, stat, stat, pltpu.VMEM((2 * tq, hw), F32)],
        compiler_params=_params(("parallel", "parallel", "arbitrary")),
        name="attn_prompt",
    )(dq, kb, vb, *lams, g)


def _attn_sample_kernel(q_ref, kn_ref, vn_ref, kc_ref, vc_ref, lq1_ref, lk1_ref, lq2_ref, lk2_ref, g_ref,
                        o_ref, *, heads, lam_init):
    tq = q_ref.shape[0]
    hw = 2 * DIFF_HEAD_DIM
    past = kc_ref.shape[2]
    lam = _lambda(lq1_ref, lk1_ref, lq2_ref, lk2_ref, lam_init)
    g = g_ref[...]
    for h in range(heads):
        hs = slice(h * hw, (h + 1) * hw)
        qs = _stack_half_queries(q_ref[:, hs])
        k_past = kc_ref[0, hs, :].astype(BF16)
        v_past = vc_ref[0, pl.ds(h, past, stride=heads), :].astype(BF16)
        s_past = jnp.dot(qs, k_past, preferred_element_type=F32)
        s_new = lax.dot_general(qs, kn_ref[:, hs], NT_DIMS, preferred_element_type=F32)
        m = jnp.maximum(jnp.max(s_past, -1, keepdims=True), jnp.max(s_new, -1, keepdims=True))
        p_past = jnp.exp2(s_past - m)
        p_new = jnp.exp2(s_new - m)
        l = jnp.sum(p_past, -1, keepdims=True) + jnp.sum(p_new, -1, keepdims=True)
        acc = (jnp.dot(p_past.astype(BF16), v_past, preferred_element_type=F32)
               + jnp.dot(p_new.astype(BF16), vn_ref[:, hs], preferred_element_type=F32))
        o_ref[:, hs] = _diff_finish(acc, l, lam, g, lam_init, tq)


def _attn_sample(dq, kb, vb, k_cache, v_cache, lams, g, *, nb, t, lam_init):
    rows, dqk = dq.shape
    hw = 2 * DIFF_HEAD_DIM
    heads = dqk // hw
    past = k_cache.shape[2]
    q_pos = past + np.arange(t)
    assert ((q_pos // CHUNK + 1) * CHUNK >= past + t).all(), "sample queries must see every cached and new key"
    new_spec = pl.BlockSpec((t, dqk), lambda b: (b, 0))
    k_cache_spec = pl.BlockSpec((1, dqk, past), lambda b: (b, 0, 0))
    v_cache_spec = pl.BlockSpec((1, past * heads, hw), lambda b: (b, 0, 0))
    small = [_resident(a.shape) for a in (*lams, g)]
    return pl.pallas_call(
        functools.partial(_attn_sample_kernel, heads=heads, lam_init=lam_init),
        grid=(nb,),
        in_specs=[new_spec, new_spec, new_spec, k_cache_spec, v_cache_spec, *small],
        out_specs=new_spec,
        out_shape=jax.ShapeDtypeStruct((rows, dqk), F32),
        compiler_params=_params(("parallel",)),
        name="attn_sample",
    )(dq, kb, vb, k_cache, v_cache, *lams, g)


def _gelu_tanh(x):
    return x * (0.5 * (1.0 + jnp.tanh(math.sqrt(2.0 / math.pi) * (x + 0.044715 * (x * x * x)))))


def _ffn_kernel(*refs, tm, d_ff, fc, alpha, blocks_per_seq, seg):
    sample = seg is not None
    if sample:
        (x_ref, og_ref, od_ref, wg_ref, wo_ref, l1g_ref, l1b_ref, wup_ref, cw_ref, cb_ref, wdn_ref,
         l2g_ref, l2b_ref, past_ref, sel1_ref, sel2_ref, y_ref, a_ref,
         a_scr, x1_scr, x1b_scr, g_scr, h_scr, acc_scr) = refs
    else:
        (x_ref, og_ref, od_ref, wg_ref, wo_ref, l1g_ref, l1b_ref, wup_ref, cw_ref, cb_ref, wdn_ref,
         l2g_ref, l2b_ref, y_ref, a_ref,
         a_scr, x1_scr, x1b_scr, g_scr, h_scr, acc_scr) = refs
    d = x_ref.shape[1]
    pad = SUBLANES
    if sample:
        a_scr[0:pad, :] = jnp.zeros((pad, d_ff), F32)
        r = lax.broadcasted_iota(jnp.int32, (tm, fc), 0) % seg
    else:
        first = pl.program_id(0) % blocks_per_seq == 0

        @pl.when(first)
        def _():
            a_scr[0:pad, :] = jnp.zeros((pad, d_ff), F32)

        @pl.when(jnp.logical_not(first))
        def _():
            a_scr[0:pad, :] = a_scr[tm:tm + pad, :]

    xb_scr, mix_scr = x1b_scr, acc_scr
    xb_scr[...] = x_ref[...].astype(BF16)
    gate_a = jnp.dot(xb_scr[...], wg_ref[:, 0:d], preferred_element_type=F32)
    mix_scr[...] = _sigmoid(gate_a) * og_ref[...]
    gate_b = jnp.dot(xb_scr[...], wg_ref[:, d:2 * d], preferred_element_type=F32)
    xb_scr[...] = (mix_scr[...] + _sigmoid(gate_b) * od_ref[...]).astype(BF16)
    x1 = _layer_norm(alpha * x_ref[...] + jnp.dot(xb_scr[...], wo_ref[...], preferred_element_type=F32),
                     l1g_ref[...], l1b_ref[...])
    x1_scr[...] = x1
    x1b_scr[...] = x1.astype(BF16)

    def up(c):
        a_scr[pad:pad + tm, c * fc:(c + 1) * fc] = jnp.dot(
            x1b_scr[...], wup_ref[:, c * fc:(c + 1) * fc], preferred_element_type=F32)
        g_scr[c % 2] = jnp.dot(x1b_scr[...], wup_ref[:, d_ff + c * fc:d_ff + (c + 1) * fc],
                               preferred_element_type=F32)

    n_chunks = d_ff // fc
    up(0)
    for c in range(n_chunks):
        cs = slice(c * fc, (c + 1) * fc)
        if c + 1 < n_chunks:
            up(c + 1)
        a = a_scr[pad:pad + tm, cs]
        gate = g_scr[c % 2]
        a1 = a_scr[pad - 1:pad - 1 + tm, cs]
        a2 = a_scr[pad - 2:pad - 2 + tm, cs]
        if sample:
            past = past_ref[:, cs]
            hist1 = jnp.dot(sel1_ref[...], past, precision=lax.Precision.HIGHEST, preferred_element_type=F32)
            hist2 = jnp.dot(sel2_ref[...], past, precision=lax.Precision.HIGHEST, preferred_element_type=F32)
            a1 = jnp.where(r == 0, 0.0, a1) + hist1
            a2 = jnp.where(r < 2, 0.0, a2) + hist2
        conv = cb_ref[:, cs] + a2 * cw_ref[0:1, cs] + a1 * cw_ref[1:2, cs] + a * cw_ref[2:3, cs]
        h_scr[c % 2] = (_gelu_tanh(conv) * gate).astype(BF16)
        down = jnp.dot(h_scr[c % 2], wdn_ref[cs, :], preferred_element_type=F32)
        if c == 0:
            acc_scr[...] = down
        else:
            acc_scr[...] += down

    if sample:
        a_ref[...] = a_scr[pad:pad + tm, :]
    else:
        a_ref[0] = a_scr[tm:tm + pad, :]
    y_ref[...] = _layer_norm(alpha * x1_scr[...] + acc_scr[...], l2g_ref[...], l2b_ref[...])


def _ffn(x, og, od, w_gate, w_out, l1g, l1b, w_up, conv_w, conv_b, w_down, l2g, l2b,
         *, tm, alpha, seq_len, past=None, fc=256):
    rows, d = x.shape
    d_ff = w_down.shape[0]
    tm = min(tm, rows)
    nblk = rows // tm
    sample = past is not None
    row_spec = pl.BlockSpec((tm, d), lambda i: (i, 0))
    weights = [w_gate, w_out, l1g, l1b, w_up, conv_w, conv_b, w_down, l2g, l2b]
    args = [x, og, od, *weights]
    scratch = [pltpu.VMEM((tm + SUBLANES, d_ff), F32), pltpu.VMEM((tm, d), F32), pltpu.VMEM((tm, d), BF16),
               pltpu.VMEM((2, tm, fc), F32), pltpu.VMEM((2, tm, fc), BF16), pltpu.VMEM((tm, d), F32)]
    if sample:
        nseq = tm // seq_len
        sel1 = np.zeros((tm, past.shape[0]), np.float32)
        sel2 = np.zeros((tm, past.shape[0]), np.float32)
        for s in range(nseq):
            sel1[s * seq_len, 2 * s + 1] = 1.0
            sel2[s * seq_len, 2 * s] = 1.0
            sel2[s * seq_len + 1, 2 * s + 1] = 1.0
        assert nblk == 1
        args += [past, jnp.asarray(sel1), jnp.asarray(sel2)]
        a_shape = jax.ShapeDtypeStruct((rows, d_ff), F32)
        a_spec = pl.BlockSpec((tm, d_ff), lambda i: (i, 0))
        kern = functools.partial(_ffn_kernel, tm=tm, d_ff=d_ff, fc=fc, alpha=alpha, blocks_per_seq=None, seg=seq_len)
    else:
        a_shape = jax.ShapeDtypeStruct((nblk, SUBLANES, d_ff), F32)
        a_spec = pl.BlockSpec((1, SUBLANES, d_ff), lambda i: (i, 0, 0))
        kern = functools.partial(_ffn_kernel, tm=tm, d_ff=d_ff, fc=fc, alpha=alpha,
                                 blocks_per_seq=seq_len // tm, seg=None)
    in_specs = [row_spec, row_spec, row_spec] + [_resident(a.shape) for a in args[3:]]
    return pl.pallas_call(
        kern,
        grid=(nblk,),
        in_specs=in_specs,
        out_specs=[row_spec, a_spec],
        out_shape=[jax.ShapeDtypeStruct((rows, d), F32), a_shape],
        scratch_shapes=scratch,
        compiler_params=_params(("arbitrary",)),
        name="ffn",
    )(*args)


def kernel(x_prompt, x_sample, cache_diff_k, cache_diff_v, state_gla, cache_ffn_conv, w_in, w_a2, b_a, gla_norm_g, lam_q1, lam_k1, lam_q2, lam_k2, diff_norm_g, w_out, ln1_g, ln1_b, w_up, conv_w, conv_b, w_down, ln2_g, ln2_b):
    depth = w_in.shape[0]
    assert depth == 1, "single-layer step"
    nbp, tp, d = x_prompt.shape
    nbs, ts, _ = x_sample.shape
    past_len = cache_diff_k.shape[2]
    qk = w_a2.shape[2]
    vd = GLA_HEADS * gla_norm_g.shape[1]
    dqk = cache_diff_v.shape[3] * cache_diff_v.shape[4]
    d_ff = w_down.shape[1]
    alpha = (2.0 * depth) ** 0.25
    lam_init = 0.8 - 0.6 * math.exp(-0.3 * 0)

    w = w_in[0]
    o_low = 2 * qk + 2 * vd
    o_diff = o_low + GLA_RANK
    o_gate = o_diff + 3 * dqk
    w_gla = w[:, :o_low].astype(BF16)
    w_low = jnp.pad(w[:, o_low:o_diff], ((0, 0), (0, LANES - GLA_RANK))).astype(BF16)
    w_q = w[:, o_diff:o_diff + dqk].astype(BF16)
    w_k = w[:, o_diff + dqk:o_diff + 2 * dqk].astype(BF16)
    w_v = w[:, o_diff + 2 * dqk:o_gate].astype(BF16)
    w_gate = w[:, o_gate:].astype(BF16)
    w_a2p = jnp.pad(w_a2[0], ((0, LANES - GLA_RANK), (0, 0))).astype(BF16)
    row2 = lambda a: a.reshape(1, -1)
    lams = [row2(lam_q1[0]), row2(lam_k1[0]), row2(lam_q2[0]), row2(lam_k2[0])]
    ffn_w = (w_gate, w_out[0].astype(BF16), row2(ln1_g[0]), row2(ln1_b[0]), w_up[0].astype(BF16), conv_w[0],
             row2(conv_b[0]), w_down[0].astype(BF16), row2(ln2_g[0]), row2(ln2_b[0]))
    gn = row2(gla_norm_g[0])
    dg = row2(diff_norm_g[0])
    proj = functools.partial(_proj, w_gla=w_gla, w_low=w_low, w_a2=w_a2p, b_a=row2(b_a[0]), w_q=w_q, w_v=w_v,
                             qk=qk, vd=vd, dqk=dqk, tm=512)
    heads = dqk // (2 * DIFF_HEAD_DIM)

    xp = x_prompt.reshape(nbp * tp, d)
    gq, gk, gv, gg, la, dq, kt_p, dv_p, kb, vb = proj(xp, w_k=w_k.T, kt_frames=tp)
    dk_p = kt_p.reshape(1, nbp, heads, 2, DIFF_HEAD_DIM, tp).transpose(0, 1, 5, 2, 3, 4)
    og, state_p = _gla(gq, gk, gv, gg, la, gn, None, nb=nbp, t=tp, c=128, nsub=4)
    od = _attn_prompt(dq, kb, vb, lams, dg, nb=nbp, t=tp, tile=512, lam_init=lam_init)
    y_p, a_last = _ffn(xp, og, od, *ffn_w, tm=256, alpha=alpha, seq_len=tp)
    conv_p = a_last.reshape(nbp, -1, SUBLANES, d_ff)[:, -1, SUBLANES - (CONV_W - 1):, :]

    xs = x_sample.reshape(nbs * ts, d)
    gq, gk, gv, gg, la, dq, dk_s, dv_s, kb, vb = proj(xs, w_k=w_k, tm=128)
    og, state_s = _gla(gq, gk, gv, gg, la, gn, state_gla[0], nb=nbs, t=ts, c=ts, nseq=math.gcd(nbs, 4))
    k_cache_t = cache_diff_k[0].transpose(0, 2, 3, 4, 1).reshape(nbs, dqk, past_len)
    v_cache = cache_diff_v[0].reshape(nbs, past_len * heads, 2 * DIFF_HEAD_DIM)
    od = _attn_sample(dq, kb, vb, k_cache_t, v_cache, lams, dg, nb=nbs, t=ts, lam_init=lam_init)
    y_s, a_s = _ffn(xs, og, od, *ffn_w, tm=nbs * ts, alpha=alpha, seq_len=ts,
                    past=cache_ffn_conv[0].reshape(nbs * (CONV_W - 1), d_ff))
    conv_s = a_s.reshape(nbs, ts, d_ff)[:, ts - (CONV_W - 1):, :]

    v_shape = lambda nb, t: (1, nb, t, heads, 2 * DIFF_HEAD_DIM)
    return (y_p.reshape(nbp, tp, d), y_s.reshape(nbs, ts, d),
            dk_p, dv_p.reshape(v_shape(nbp, tp)), state_p[None], conv_p[None],
            dk_s.reshape(1, nbs, ts, heads, 2, DIFF_HEAD_DIM), dv_s.reshape(v_shape(nbs, ts)), state_s[None],
            conv_s[None])
```

```python
import functools
import math

import numpy as np
import jax
import jax.numpy as jnp
from jax import lax
from jax.experimental import pallas as pl
from jax.experimental.pallas import tpu as pltpu

F32 = jnp.float32
BF16 = jnp.bfloat16

CHUNK = 64
GLA_HEADS = 4
GLA_RANK = 16
GLA_TAU = 16.0
GLA_BLOCK = 16
DIFF_HEAD_DIM = 64
CONV_W = 3
EPS = 1e-5
LOG2_E = math.log2(math.e)

LANES = 128
SUBLANES = 8
VMEM_LIMIT = 56 * 1024 * 1024

PROJ_ROWS = 512
PROJ_ROWS_SAMPLE = 128
FFN_ROWS = 256
FFN_CHUNK = 256
GLA_CHUNK = 128
GLA_CHUNKS_PER_STEP = 8
GLA_SAMPLE_SEQS_PER_STEP = 4
ATTN_TILE = 512
TILE_GROUPS = (8, 4, 2, 1)

NT_DIMS = (((1,), (1,)), ((), ()))
TN_DIMS = (((0,), (0,)), ((), ()))


def _sigmoid(x):
    return 1.0 / (1.0 + jnp.exp(-x))


def _layer_norm(x, g, b):
    mu = jnp.mean(x, -1, keepdims=True)
    xc = x - mu
    var = jnp.mean(xc * xc, -1, keepdims=True)
    return xc * lax.rsqrt(var + EPS) * g + b


def _resident(shape):
    nd = len(shape)
    return pl.BlockSpec(shape, lambda *_: (0,) * nd, pipeline_mode=pl.Buffered(1))


def _params(sem):
    return pltpu.CompilerParams(dimension_semantics=sem, vmem_limit_bytes=VMEM_LIMIT)


def _proj_kernel(x_ref, wgla_ref, wlow_ref, wa2_ref, ba_ref, wq_ref, wk_ref, wv_ref,
                 gq_ref, gk_ref, gv_ref, gg_ref, la_ref, dq_ref, dk_ref, dv_ref, kb_ref, vb_ref,
                 *, qk, vd, dqk, k_transposed):
    xb = x_ref[...].astype(BF16)

    def mm(w_ref, lo, hi):
        return jnp.dot(xb, w_ref[:, lo:hi], preferred_element_type=F32)

    gq_ref[...] = mm(wgla_ref, 0, qk)
    gk_ref[...] = mm(wgla_ref, qk, 2 * qk)
    gv_ref[...] = mm(wgla_ref, 2 * qk, 2 * qk + vd).astype(BF16)
    gg_ref[...] = mm(wgla_ref, 2 * qk + vd, 2 * qk + 2 * vd)

    g_low = jnp.dot(xb, wlow_ref[...], preferred_element_type=F32)
    z = jnp.dot(g_low.astype(BF16), wa2_ref[...], preferred_element_type=F32) + ba_ref[...]
    softplus_neg = jnp.maximum(-z, 0.0) + jnp.log1p(jnp.exp(-jnp.abs(z)))
    la_ref[...] = -softplus_neg * (LOG2_E / GLA_TAU)

    dq_ref[...] = (mm(wq_ref, 0, dqk) * (DIFF_HEAD_DIM ** -0.5 * LOG2_E)).astype(BF16)
    if k_transposed:
        dk = lax.dot_general(wk_ref[...], xb, NT_DIMS, preferred_element_type=F32)
        dk_ref[0] = dk
        kb_ref[0] = dk.astype(BF16)
    else:
        dk = mm(wk_ref, 0, dqk)
        dk_ref[...] = dk
        kb_ref[...] = dk.astype(BF16)
    dv = mm(wv_ref, 0, dqk)
    dv_ref[...] = dv
    vb_ref[...] = dv.astype(BF16)


def _proj(x, w_gla, w_low, w_a2, b_a, w_q, w_k, w_v, *, qk, vd, dqk, tm, kt_frames=None):
    rows, d = x.shape
    tm = min(tm, rows)
    row_spec = lambda n: pl.BlockSpec((tm, n), lambda i: (i, 0))
    out = lambda n, dt: jax.ShapeDtypeStruct((rows, n), dt)
    if kt_frames is None:
        k_spec, k_out = row_spec(dqk), lambda dt: out(dqk, dt)
    else:
        per_seq = kt_frames // tm
        k_spec = pl.BlockSpec((1, dqk, tm), lambda i: (i // per_seq, 0, i % per_seq))
        k_out = lambda dt: jax.ShapeDtypeStruct((rows // kt_frames, dqk, kt_frames), dt)
    weights = (w_gla, w_low, w_a2, b_a, w_q, w_k, w_v)
    return pl.pallas_call(
        functools.partial(_proj_kernel, qk=qk, vd=vd, dqk=dqk, k_transposed=kt_frames is not None),
        grid=(rows // tm,),
        in_specs=[row_spec(d)] + [_resident(a.shape) for a in weights],
        out_specs=[row_spec(qk), row_spec(qk), row_spec(vd), row_spec(vd), row_spec(qk),
                   row_spec(dqk), k_spec, row_spec(dqk), k_spec, row_spec(dqk)],
        out_shape=[out(qk, F32), out(qk, F32), out(vd, BF16), out(vd, F32), out(qk, F32),
                   out(dqk, BF16), k_out(F32), out(dqk, F32), k_out(BF16), out(dqk, BF16)],
        compiler_params=_params(("parallel",)),
        name="proj",
    )(x, *weights)


def _gla_levels(c):
    d = min(c, 2 * GLA_BLOCK)
    off = []
    b = c // 2
    while b >= d:
        off.append(b)
        b //= 2
    return off, d


def _row_broadcast_refs(cum, group, ref_offset):
    c, n = cum.shape
    parts = [jnp.broadcast_to(cum[p * group + ref_offset:p * group + ref_offset + 1, :], (group, n))
             for p in range(c // group)]
    return parts[0] if len(parts) == 1 else jnp.concatenate(parts, axis=0)


def _gla_kernel(*refs, c, nsub, nseq, dk, dv, has_s0):
    if has_s0:
        q_ref, k_ref, v_ref, gg_ref, la_ref, gn_ref, s0_ref, o_ref, sout_ref, s_scr = refs
    else:
        q_ref, k_ref, v_ref, gg_ref, la_ref, gn_ref, o_ref, sout_ref, s_scr = refs
    step = pl.program_id(1)

    @pl.when(step == 0)
    def _():
        s_scr[...] = s0_ref[...] if has_s0 else jnp.zeros_like(s_scr)

    for seq in range(nseq):
        for sub in range(nsub):
            first = (seq * nsub + sub) * c
            _gla_chunk(q_ref, k_ref, v_ref, gg_ref, la_ref, gn_ref, o_ref, s_scr.at[seq],
                       slice(first, first + c), c=c, dk=dk, dv=dv)

    @pl.when(step == pl.num_programs(1) - 1)
    def _():
        sout_ref[...] = s_scr[...]


def _gla_chunk(q_ref, k_ref, v_ref, gg_ref, la_ref, gn_ref, o_ref, s_scr, rows, *, c, dk, dv):
    row = lax.broadcasted_iota(jnp.int32, (c, c), 0)
    col = lax.broadcasted_iota(jnp.int32, (c, c), 1)
    tri = (col <= row).astype(F32)
    cum_all = jnp.dot(tri, la_ref[rows, :], precision=lax.Precision.HIGHEST, preferred_element_type=F32)

    off_levels, d = _gla_levels(c)
    scale = dk ** -0.5
    gn = gn_ref[...]

    for h in range(GLA_HEADS):
        ks = slice(h * dk, (h + 1) * dk)
        vs = slice(h * dv, (h + 1) * dv)
        cum = cum_all[:, ks]
        qs = q_ref[rows, ks] * scale
        kk = k_ref[rows, ks]
        vv = v_ref[rows, vs]

        ref = _row_broadcast_refs(cum, d, d // 2 - 1)
        qd = (qs * jnp.exp2(cum - ref)).astype(BF16)
        kd = (kk * jnp.exp2(ref - cum)).astype(BF16)
        att = lax.dot_general(qd, kd, NT_DIMS, preferred_element_type=F32)
        att = jnp.where((row // d == col // d) & (col <= row), att, 0.0)
        for b in off_levels:
            ref = _row_broadcast_refs(cum, 2 * b, b - 1)
            ql = (qs * jnp.exp2(jnp.minimum(cum - ref, 0.0))).astype(BF16)
            kl = (kk * jnp.exp2(jnp.minimum(ref - cum, 0.0))).astype(BF16)
            a_l = lax.dot_general(ql, kl, NT_DIMS, preferred_element_type=F32)
            att = jnp.where(((row // b) % 2 == 1) & (col // b == row // b - 1), a_l, att)

        s_h = s_scr[h]
        q_dec = (qs * jnp.exp2(cum)).astype(BF16)
        o = (jnp.dot(att.astype(BF16), vv, preferred_element_type=F32)
             + jnp.dot(q_dec, s_h.astype(BF16), preferred_element_type=F32))

        last = cum[c - 1:c, :]
        k_end = (kk * jnp.exp2(last - cum)).astype(BF16)
        kv = lax.dot_general(k_end, vv, TN_DIMS, preferred_element_type=F32)
        decay_col = jnp.transpose(jnp.broadcast_to(jnp.exp2(last), (dk, dk)))
        decay = jnp.concatenate([decay_col] * (dv // dk), axis=1)
        s_scr[h] = decay * s_h + kv

        ms = jnp.mean(o * o, -1, keepdims=True)
        gate = gg_ref[rows, vs]
        o_ref[rows, vs] = o * lax.rsqrt(ms + EPS) * gn * (gate * _sigmoid(gate))


def _gla(gq, gk, gv, gg, la, gn, s0, *, nb, t, c, nsub=1, nseq=1):
    rows, qk = gq.shape
    vd = gv.shape[1]
    dk, dv = qk // GLA_HEADS, vd // GLA_HEADS
    c = min(c, t)
    nsub = min(nsub, t // c)
    nc = t // (c * nsub)
    assert nseq == 1 or (nc == 1 and nb % nseq == 0), "several sequences per step only when a step spans them whole"
    blk = lambda n: pl.BlockSpec((nseq * c * nsub, n), lambda b, s: (b * nc + s, 0))
    state_spec = pl.BlockSpec((nseq, GLA_HEADS, dk, dv), lambda b, s: (b, 0, 0, 0))
    in_specs = [blk(qk), blk(qk), blk(vd), blk(vd), blk(qk), _resident(gn.shape)]
    args = [gq, gk, gv, gg, la, gn]
    if s0 is not None:
        in_specs.append(state_spec)
        args.append(s0)
    return pl.pallas_call(
        functools.partial(_gla_kernel, c=c, nsub=nsub, nseq=nseq, dk=dk, dv=dv, has_s0=s0 is not None),
        grid=(nb // nseq, nc),
        in_specs=in_specs,
        out_specs=[blk(vd), state_spec],
        out_shape=[jax.ShapeDtypeStruct((rows, vd), F32),
                   jax.ShapeDtypeStruct((nb, GLA_HEADS, dk, dv), F32)],
        scratch_shapes=[pltpu.VMEM((nseq, GLA_HEADS, dk, dv), F32)],
        compiler_params=_params(("parallel", "arbitrary")),
        name="gla",
    )(*args)


def _stack_half_queries(q):
    lane = lax.broadcasted_iota(jnp.int32, q.shape, 1)
    zero = jnp.zeros_like(q)
    return jnp.concatenate([jnp.where(lane < DIFF_HEAD_DIM, q, zero),
                            jnp.where(lane >= DIFF_HEAD_DIM, q, zero)], axis=0)


def _lambda(lq1_ref, lk1_ref, lq2_ref, lk2_ref, lam_init):
    return (jnp.exp(jnp.sum(lq1_ref[...] * lk1_ref[...], -1, keepdims=True))
            - jnp.exp(jnp.sum(lq2_ref[...] * lk2_ref[...], -1, keepdims=True)) + lam_init)


def _diff_finish(acc, l, lam, g, lam_init, tq):
    o = acc * (1.0 / l)
    od = o[:tq] - lam * o[tq:]
    ms = jnp.mean(od * od, -1, keepdims=True)
    return od * lax.rsqrt(ms + EPS) * g * (1.0 - lam_init)


def _attn_prompt_kernel(q_ref, k_ref, v_ref, lq1_ref, lk1_ref, lq2_ref, lk2_ref, g_ref, o_ref,
                        qs_scr, s_scr, p_scr, m_scr, l_scr, a_scr, acc_scr, *, tq, tk, lam_init):
    i = pl.program_id(2)
    q_first = i * tq
    qs_scr[...] = _stack_half_queries(q_ref[...])
    m_scr[...] = jnp.full(m_scr.shape, -jnp.inf, F32)
    l_scr[...] = jnp.zeros(l_scr.shape, F32)
    acc_scr[...] = jnp.zeros(acc_scr.shape, F32)
    reps = tk // LANES

    def tile(j, carry, diagonal):
        start = pl.multiple_of(j * tk, tk)
        s_scr[...] = jnp.dot(qs_scr[...], k_ref[:, pl.ds(start, tk)],
                             preferred_element_type=F32)
        for r0 in range(0, 2 * tq, CHUNK):
            rows = slice(r0, r0 + CHUNK)
            limit = ((r0 % tq) // CHUNK + 1) * CHUNK if diagonal else tk
            pieces = []
            for g in range(reps):
                lo = g * LANES
                if lo >= limit:
                    pieces.append(None)
                elif lo + LANES <= limit:
                    pieces.append(s_scr[rows, lo:lo + LANES])
                else:
                    lane = lax.broadcasted_iota(jnp.int32, (CHUNK, LANES), 1)
                    pieces.append(jnp.where(lane < limit - lo, s_scr[rows, lo:lo + LANES], -jnp.inf))
            seen = [x for x in pieces if x is not None]
            m_prev = m_scr[rows, :]
            m_next = jnp.maximum(m_prev, jnp.max(functools.reduce(jnp.maximum, seen), -1, keepdims=True))
            alpha = jnp.exp2(m_prev - m_next)
            p_lanes = None
            for g, x in enumerate(pieces):
                lo = g * LANES
                if x is None:
                    p_scr[rows, lo:lo + LANES] = jnp.zeros((CHUNK, LANES), BF16)
                    continue
                p = jnp.exp2(x - m_next)
                p_lanes = p if p_lanes is None else p_lanes + p
                p_scr[rows, lo:lo + LANES] = p.astype(BF16)
            l_scr[rows, :] = alpha * l_scr[rows, :] + p_lanes
            m_scr[rows, :] = m_next
            a_scr[rows, :] = alpha
        acc_scr[...] = a_scr[...] * acc_scr[...] + jnp.dot(p_scr[...], v_ref[pl.ds(start, tk), :],
                                                           preferred_element_type=F32)
        return carry

    n_full = q_first // tk
    def tile_group(jj, carry, first, size):
        for u in range(size):
            tile(first + size * jj + u, carry, diagonal=False)
        return carry

    done = 0
    for size in TILE_GROUPS:
        trips = (n_full - done) // size
        lax.fori_loop(0, trips, functools.partial(tile_group, first=done, size=size), 0)
        done = done + trips * size
    tile(n_full, 0, diagonal=True)

    lam = _lambda(lq1_ref, lk1_ref, lq2_ref, lk2_ref, lam_init)
    l = jnp.sum(l_scr[...], -1, keepdims=True)
    o_ref[...] = _diff_finish(acc_scr[...], l, lam, g_ref[...], lam_init, tq)


def _attn_prompt(dq, kb, vb, lams, g, *, nb, t, tile, lam_init):
    rows, dqk = dq.shape
    hw = 2 * DIFF_HEAD_DIM
    heads = dqk // hw
    tq = tk = min(tile, t)
    assert tq % CHUNK == 0 and t % tq == 0
    nq = t // tq
    k_spec = pl.BlockSpec((None, hw, t), lambda b, h, i: (b, h, 0))
    v_spec = pl.BlockSpec((t, hw), lambda b, h, i: (b, h))
    q_spec = pl.BlockSpec((tq, hw), lambda b, h, i: (b * nq + i, h))
    small = [_resident(a.shape) for a in (*lams, g)]
    stat = pltpu.VMEM((2 * tq, LANES), F32)
    return pl.pallas_call(
        functools.partial(_attn_prompt_kernel, tq=tq, tk=tk, lam_init=lam_init),
        grid=(nb, heads, nq),
        in_specs=[q_spec, k_spec, v_spec, *small],
        out_specs=q_spec,
        out_shape=jax.ShapeDtypeStruct((rows, dqk), F32),
        scratch_shapes=[pltpu.VMEM((2 * tq, hw), BF16), pltpu.VMEM((2 * tq, tk), F32),
                        pltpu.VMEM((2 * tq, tk), BF16), stat, stat, stat, pltpu.VMEM((2 * tq, hw), F32)],
        compiler_params=_params(("parallel", "parallel", "arbitrary")),
        name="attn_prompt",
    )(dq, kb, vb, *lams, g)


def _attn_sample_kernel(q_ref, kn_ref, vn_ref, kc_ref, vc_ref, lq1_ref, lk1_ref, lq2_ref, lk2_ref, g_ref,
                        o_ref, *, heads, lam_init):
    tq = q_ref.shape[0]
    hw = 2 * DIFF_HEAD_DIM
    past = kc_ref.shape[2]
    lam = _lambda(lq1_ref, lk1_ref, lq2_ref, lk2_ref, lam_init)
    g = g_ref[...]
    for h in range(heads):
        hs = slice(h * hw, (h + 1) * hw)
        qs = _stack_half_queries(q_ref[:, hs])
        k_past = kc_ref[0, hs, :].astype(BF16)
        v_past = vc_ref[0, pl.ds(h, past, stride=heads), :].astype(BF16)
        s_past = jnp.dot(qs, k_past, preferred_element_type=F32)
        s_new = lax.dot_general(qs, kn_ref[:, hs], NT_DIMS, preferred_element_type=F32)
        m = jnp.maximum(jnp.max(s_past, -1, keepdims=True), jnp.max(s_new, -1, keepdims=True))
        p_past = jnp.exp2(s_past - m)
        p_new = jnp.exp2(s_new - m)
        l = jnp.sum(p_past, -1, keepdims=True) + jnp.sum(p_new, -1, keepdims=True)
        acc = (jnp.dot(p_past.astype(BF16), v_past, preferred_element_type=F32)
               + jnp.dot(p_new.astype(BF16), vn_ref[:, hs], preferred_element_type=F32))
        o_ref[:, hs] = _diff_finish(acc, l, lam, g, lam_init, tq)


def _attn_sample(dq, kb, vb, k_cache, v_cache, lams, g, *, nb, t, lam_init):
    rows, dqk = dq.shape
    hw = 2 * DIFF_HEAD_DIM
    heads = dqk // hw
    past = k_cache.shape[2]
    q_pos = past + np.arange(t)
    assert ((q_pos // CHUNK + 1) * CHUNK >= past + t).all(), "sample queries must see every cached and new key"
    new_spec = pl.BlockSpec((t, dqk), lambda b: (b, 0))
    k_cache_spec = pl.BlockSpec((1, dqk, past), lambda b: (b, 0, 0))
    v_cache_spec = pl.BlockSpec((1, past * heads, hw), lambda b: (b, 0, 0))
    small = [_resident(a.shape) for a in (*lams, g)]
    return pl.pallas_call(
        functools.partial(_attn_sample_kernel, heads=heads, lam_init=lam_init),
        grid=(nb,),
        in_specs=[new_spec, new_spec, new_spec, k_cache_spec, v_cache_spec, *small],
        out_specs=new_spec,
        out_shape=jax.ShapeDtypeStruct((rows, dqk), F32),
        compiler_params=_params(("parallel",)),
        name="attn_sample",
    )(dq, kb, vb, k_cache, v_cache, *lams, g)


def _gelu_tanh(x):
    return x * (0.5 * (1.0 + jnp.tanh(math.sqrt(2.0 / math.pi) * (x + 0.044715 * (x * x * x)))))


def _ffn_kernel(*refs, tm, d_ff, fc, alpha, blocks_per_seq, seg):
    sample = seg is not None
    if sample:
        (x_ref, og_ref, od_ref, wg_ref, wo_ref, l1g_ref, l1b_ref, wup_ref, cw_ref, cb_ref, wdn_ref,
         l2g_ref, l2b_ref, past_ref, sel1_ref, sel2_ref, y_ref, a_ref,
         a_scr, x1_scr, x1b_scr, g_scr, h_scr, acc_scr) = refs
    else:
        (x_ref, og_ref, od_ref, wg_ref, wo_ref, l1g_ref, l1b_ref, wup_ref, cw_ref, cb_ref, wdn_ref,
         l2g_ref, l2b_ref, y_ref, a_ref,
         a_scr, x1_scr, x1b_scr, g_scr, h_scr, acc_scr) = refs
    d = x_ref.shape[1]
    pad = SUBLANES
    if sample:
        a_scr[0:pad, :] = jnp.zeros((pad, d_ff), F32)
        r = lax.broadcasted_iota(jnp.int32, (tm, fc), 0) % seg
    else:
        first = pl.program_id(0) % blocks_per_seq == 0

        @pl.when(first)
        def _():
            a_scr[0:pad, :] = jnp.zeros((pad, d_ff), F32)

        @pl.when(jnp.logical_not(first))
        def _():
            a_scr[0:pad, :] = a_scr[tm:tm + pad, :]

    xb_scr, mix_scr = x1b_scr, acc_scr
    xb_scr[...] = x_ref[...].astype(BF16)
    gate_a = jnp.dot(xb_scr[...], wg_ref[:, 0:d], preferred_element_type=F32)
    mix_scr[...] = _sigmoid(gate_a) * og_ref[...]
    gate_b = jnp.dot(xb_scr[...], wg_ref[:, d:2 * d], preferred_element_type=F32)
    xb_scr[...] = (mix_scr[...] + _sigmoid(gate_b) * od_ref[...]).astype(BF16)
    x1 = _layer_norm(alpha * x_ref[...] + jnp.dot(xb_scr[...], wo_ref[...], preferred_element_type=F32),
                     l1g_ref[...], l1b_ref[...])
    x1_scr[...] = x1
    x1b_scr[...] = x1.astype(BF16)

    def up(c):
        a_scr[pad:pad + tm, c * fc:(c + 1) * fc] = jnp.dot(
            x1b_scr[...], wup_ref[:, c * fc:(c + 1) * fc], preferred_element_type=F32)
        g_scr[c % 2] = jnp.dot(x1b_scr[...], wup_ref[:, d_ff + c * fc:d_ff + (c + 1) * fc],
                               preferred_element_type=F32)

    n_chunks = d_ff // fc
    up(0)
    for c in range(n_chunks):
        cs = slice(c * fc, (c + 1) * fc)
        if c + 1 < n_chunks:
            up(c + 1)
        a = a_scr[pad:pad + tm, cs]
        gate = g_scr[c % 2]
        a1 = a_scr[pad - 1:pad - 1 + tm, cs]
        a2 = a_scr[pad - 2:pad - 2 + tm, cs]
        if sample:
            past = past_ref[:, cs]
            hist1 = jnp.dot(sel1_ref[...], past, precision=lax.Precision.HIGHEST, preferred_element_type=F32)
            hist2 = jnp.dot(sel2_ref[...], past, precision=lax.Precision.HIGHEST, preferred_element_type=F32)
            a1 = jnp.where(r == 0, 0.0, a1) + hist1
            a2 = jnp.where(r < 2, 0.0, a2) + hist2
        conv = cb_ref[:, cs] + a2 * cw_ref[0:1, cs] + a1 * cw_ref[1:2, cs] + a * cw_ref[2:3, cs]
        h_scr[c % 2] = (_gelu_tanh(conv) * gate).astype(BF16)
        down = jnp.dot(h_scr[c % 2], wdn_ref[cs, :], preferred_element_type=F32)
        if c == 0:
            acc_scr[...] = down
        else:
            acc_scr[...] += down

    if sample:
        a_ref[...] = a_scr[pad:pad + tm, :]
    else:
        a_ref[0] = a_scr[tm:tm + pad, :]
    y_ref[...] = _layer_norm(alpha * x1_scr[...] + acc_scr[...], l2g_ref[...], l2b_ref[...])


def _ffn(x, og, od, w_gate, w_out, l1g, l1b, w_up, conv_w, conv_b, w_down, l2g, l2b,
         *, tm, alpha, seq_len, past=None, fc=FFN_CHUNK):
    rows, d = x.shape
    d_ff = w_down.shape[0]
    tm = min(tm, rows)
    nblk = rows // tm
    sample = past is not None
    row_spec = pl.BlockSpec((tm, d), lambda i: (i, 0))
    weights = [w_gate, w_out, l1g, l1b, w_up, conv_w, conv_b, w_down, l2g, l2b]
    args = [x, og, od, *weights]
    scratch = [pltpu.VMEM((tm + SUBLANES, d_ff), F32), pltpu.VMEM((tm, d), F32), pltpu.VMEM((tm, d), BF16),
               pltpu.VMEM((2, tm, fc), F32), pltpu.VMEM((2, tm, fc), BF16), pltpu.VMEM((tm, d), F32)]
    if sample:
        nseq = tm // seq_len
        sel1 = np.zeros((tm, past.shape[0]), np.float32)
        sel2 = np.zeros((tm, past.shape[0]), np.float32)
        for s in range(nseq):
            sel1[s * seq_len, 2 * s + 1] = 1.0
            sel2[s * seq_len, 2 * s] = 1.0
            sel2[s * seq_len + 1, 2 * s + 1] = 1.0
        assert nblk == 1
        args += [past, jnp.asarray(sel1), jnp.asarray(sel2)]
        a_shape = jax.ShapeDtypeStruct((rows, d_ff), F32)
        a_spec = pl.BlockSpec((tm, d_ff), lambda i: (i, 0))
        kern = functools.partial(_ffn_kernel, tm=tm, d_ff=d_ff, fc=fc, alpha=alpha, blocks_per_seq=None, seg=seq_len)
    else:
        a_shape = jax.ShapeDtypeStruct((nblk, SUBLANES, d_ff), F32)
        a_spec = pl.BlockSpec((1, SUBLANES, d_ff), lambda i: (i, 0, 0))
        kern = functools.partial(_ffn_kernel, tm=tm, d_ff=d_ff, fc=fc, alpha=alpha,
                                 blocks_per_seq=seq_len // tm, seg=None)
    in_specs = [row_spec, row_spec, row_spec] + [_resident(a.shape) for a in args[3:]]
    return pl.pallas_call(
        kern,
        grid=(nblk,),
        in_specs=in_specs,
        out_specs=[row_spec, a_spec],
        out_shape=[jax.ShapeDtypeStruct((rows, d), F32), a_shape],
        scratch_shapes=scratch,
        compiler_params=_params(("arbitrary",)),
        name="ffn",
    )(*args)


def kernel(x_prompt, x_sample, cache_diff_k, cache_diff_v, state_gla, cache_ffn_conv, w_in, w_a2, b_a, gla_norm_g, lam_q1, lam_k1, lam_q2, lam_k2, diff_norm_g, w_out, ln1_g, ln1_b, w_up, conv_w, conv_b, w_down, ln2_g, ln2_b):
    depth = w_in.shape[0]
    assert depth == 1, "single-layer step"
    nbp, tp, d = x_prompt.shape
    nbs, ts, _ = x_sample.shape
    past_len = cache_diff_k.shape[2]
    qk = w_a2.shape[2]
    vd = GLA_HEADS * gla_norm_g.shape[1]
    dqk = cache_diff_v.shape[3] * cache_diff_v.shape[4]
    d_ff = w_down.shape[1]
    alpha = (2.0 * depth) ** 0.25
    lam_init = 0.8 - 0.6 * math.exp(-0.3 * 0)

    w = w_in[0]
    o_low = 2 * qk + 2 * vd
    o_diff = o_low + GLA_RANK
    o_gate = o_diff + 3 * dqk
    w_gla = w[:, :o_low].astype(BF16)
    w_low = jnp.pad(w[:, o_low:o_diff], ((0, 0), (0, LANES - GLA_RANK))).astype(BF16)
    w_q = w[:, o_diff:o_diff + dqk].astype(BF16)
    w_k = w[:, o_diff + dqk:o_diff + 2 * dqk].astype(BF16)
    w_v = w[:, o_diff + 2 * dqk:o_gate].astype(BF16)
    w_gate = w[:, o_gate:].astype(BF16)
    w_a2p = jnp.pad(w_a2[0], ((0, LANES - GLA_RANK), (0, 0))).astype(BF16)
    row2 = lambda a: a.reshape(1, -1)
    lams = [row2(lam_q1[0]), row2(lam_k1[0]), row2(lam_q2[0]), row2(lam_k2[0])]
    ffn_w = (w_gate, w_out[0].astype(BF16), row2(ln1_g[0]), row2(ln1_b[0]), w_up[0].astype(BF16), conv_w[0],
             row2(conv_b[0]), w_down[0].astype(BF16), row2(ln2_g[0]), row2(ln2_b[0]))
    gn = row2(gla_norm_g[0])
    dg = row2(diff_norm_g[0])
    proj = functools.partial(_proj, w_gla=w_gla, w_low=w_low, w_a2=w_a2p, b_a=row2(b_a[0]), w_q=w_q, w_v=w_v,
                             qk=qk, vd=vd, dqk=dqk)
    heads = dqk // (2 * DIFF_HEAD_DIM)

    xp = x_prompt.reshape(nbp * tp, d)
    gq, gk, gv, gg, la, dq, kt_p, dv_p, kb, vb = proj(xp, w_k=w_k.T, tm=PROJ_ROWS, kt_frames=tp)
    dk_p = kt_p.reshape(1, nbp, heads, 2, DIFF_HEAD_DIM, tp).transpose(0, 1, 5, 2, 3, 4)
    og, state_p = _gla(gq, gk, gv, gg, la, gn, None, nb=nbp, t=tp, c=GLA_CHUNK, nsub=GLA_CHUNKS_PER_STEP)
    od = _attn_prompt(dq, kb, vb, lams, dg, nb=nbp, t=tp, tile=ATTN_TILE, lam_init=lam_init)
    y_p, a_last = _ffn(xp, og, od, *ffn_w, tm=FFN_ROWS, alpha=alpha, seq_len=tp)
    conv_p = a_last.reshape(nbp, -1, SUBLANES, d_ff)[:, -1, SUBLANES - (CONV_W - 1):, :]

    xs = x_sample.reshape(nbs * ts, d)
    gq, gk, gv, gg, la, dq, dk_s, dv_s, kb, vb = proj(xs, w_k=w_k, tm=PROJ_ROWS_SAMPLE)
    og, state_s = _gla(gq, gk, gv, gg, la, gn, state_gla[0], nb=nbs, t=ts, c=ts,
                       nseq=math.gcd(nbs, GLA_SAMPLE_SEQS_PER_STEP))
    k_cache_t = cache_diff_k[0].transpose(0, 2, 3, 4, 1).reshape(nbs, dqk, past_len)
    v_cache = cache_diff_v[0].reshape(nbs, past_len * heads, 2 * DIFF_HEAD_DIM)
    od = _attn_sample(dq, kb, vb, k_cache_t, v_cache, lams, dg, nb=nbs, t=ts, lam_init=lam_init)
    y_s, a_s = _ffn(xs, og, od, *ffn_w, tm=nbs * ts, alpha=alpha, seq_len=ts,
                    past=cache_ffn_conv[0].reshape(nbs * (CONV_W - 1), d_ff))
    conv_s = a_s.reshape(nbs, ts, d_ff)[:, ts - (CONV_W - 1):, :]

    v_shape = lambda nb, t: (1, nb, t, heads, 2 * DIFF_HEAD_DIM)
    return (y_p.reshape(nbp, tp, d), y_s.reshape(nbs, ts, d),
            dk_p, dv_p.reshape(v_shape(nbp, tp)), state_p[None], conv_p[None],
            dk_s.reshape(1, nbs, ts, heads, 2, DIFF_HEAD_DIM), dv_s.reshape(v_shape(nbs, ts)), state_s[None],
            conv_s[None])
```

```python
import functools
import math

import numpy as np
import jax
import jax.numpy as jnp
from jax import lax
from jax.experimental import pallas as pl
from jax.experimental.pallas import tpu as pltpu

F32 = jnp.float32
BF16 = jnp.bfloat16

CHUNK = 64
GLA_HEADS = 4
GLA_RANK = 16
GLA_TAU = 16.0
GLA_BLOCK = 16
DIFF_HEAD_DIM = 64
CONV_W = 3
EPS = 1e-5
LOG2_E = math.log2(math.e)

LANES = 128
SUBLANES = 8
VMEM_LIMIT = 56 * 1024 * 1024

PROJ_ROWS = 512
PROJ_ROWS_SAMPLE = 128
FFN_ROWS = 256
FFN_CHUNK = 256
GLA_CHUNK = 128
GLA_CHUNKS_PER_STEP = 8
GLA_SAMPLE_SEQS_PER_STEP = 4
ATTN_TILE = 512
TILE_GROUPS = (8, 4, 2, 1)

NT_DIMS = (((1,), (1,)), ((), ()))
TN_DIMS = (((0,), (0,)), ((), ()))


def _sigmoid(x):
    return 1.0 / (1.0 + jnp.exp(-x))


def _layer_norm(x, g, b):
    mu = jnp.mean(x, -1, keepdims=True)
    xc = x - mu
    var = jnp.mean(xc * xc, -1, keepdims=True)
    return xc * lax.rsqrt(var + EPS) * g + b


def _resident(shape):
    nd = len(shape)
    return pl.BlockSpec(shape, lambda *_: (0,) * nd, pipeline_mode=pl.Buffered(1))


def _params(sem):
    return pltpu.CompilerParams(dimension_semantics=sem, vmem_limit_bytes=VMEM_LIMIT)


def _proj_kernel(x_ref, wgla_ref, wlow_ref, wa2_ref, ba_ref, wq_ref, wk_ref, wv_ref,
                 gq_ref, gk_ref, gv_ref, gg_ref, la_ref, dq_ref, dk_ref, dv_ref, kb_ref, vb_ref,
                 *, qk, vd, dqk, k_transposed):
    xb = x_ref[...].astype(BF16)

    def mm(w_ref, lo, hi):
        return jnp.dot(xb, w_ref[:, lo:hi], preferred_element_type=F32)

    gq_ref[...] = mm(wgla_ref, 0, qk).astype(BF16)
    gk_ref[...] = mm(wgla_ref, qk, 2 * qk).astype(BF16)
    gv_ref[...] = mm(wgla_ref, 2 * qk, 2 * qk + vd).astype(BF16)
    gg_ref[...] = mm(wgla_ref, 2 * qk + vd, 2 * qk + 2 * vd)

    g_low = jnp.dot(xb, wlow_ref[...], preferred_element_type=F32)
    z = jnp.dot(g_low.astype(BF16), wa2_ref[...], preferred_element_type=F32) + ba_ref[...]
    softplus_neg = jnp.maximum(-z, 0.0) + jnp.log1p(jnp.exp(-jnp.abs(z)))
    la_ref[...] = -softplus_neg * (LOG2_E / GLA_TAU)

    dq_ref[...] = (mm(wq_ref, 0, dqk) * (DIFF_HEAD_DIM ** -0.5 * LOG2_E)).astype(BF16)
    if k_transposed:
        dk = lax.dot_general(wk_ref[...], xb, NT_DIMS, preferred_element_type=F32)
        dk_ref[0] = dk
        kb_ref[0] = dk.astype(BF16)
    else:
        dk = mm(wk_ref, 0, dqk)
        dk_ref[...] = dk
        kb_ref[...] = dk.astype(BF16)
    dv = mm(wv_ref, 0, dqk)
    dv_ref[...] = dv
    vb_ref[...] = dv.astype(BF16)


def _proj(x, w_gla, w_low, w_a2, b_a, w_q, w_k, w_v, *, qk, vd, dqk, tm, kt_frames=None):
    rows, d = x.shape
    tm = min(tm, rows)
    row_spec = lambda n: pl.BlockSpec((tm, n), lambda i: (i, 0))
    out = lambda n, dt: jax.ShapeDtypeStruct((rows, n), dt)
    if kt_frames is None:
        k_spec, k_out = row_spec(dqk), lambda dt: out(dqk, dt)
    else:
        per_seq = kt_frames // tm
        k_spec = pl.BlockSpec((1, dqk, tm), lambda i: (i // per_seq, 0, i % per_seq))
        k_out = lambda dt: jax.ShapeDtypeStruct((rows // kt_frames, dqk, kt_frames), dt)
    weights = (w_gla, w_low, w_a2, b_a, w_q, w_k, w_v)
    return pl.pallas_call(
        functools.partial(_proj_kernel, qk=qk, vd=vd, dqk=dqk, k_transposed=kt_frames is not None),
        grid=(rows // tm,),
        in_specs=[row_spec(d)] + [_resident(a.shape) for a in weights],
        out_specs=[row_spec(qk), row_spec(qk), row_spec(vd), row_spec(vd), row_spec(qk),
                   row_spec(dqk), k_spec, row_spec(dqk), k_spec, row_spec(dqk)],
        out_shape=[out(qk, BF16), out(qk, BF16), out(vd, BF16), out(vd, F32), out(qk, F32),
                   out(dqk, BF16), k_out(F32), out(dqk, F32), k_out(BF16), out(dqk, BF16)],
        compiler_params=_params(("parallel",)),
        name="proj",
    )(x, *weights)


def _gla_levels(c):
    d = min(c, 2 * GLA_BLOCK)
    off = []
    b = c // 2
    while b >= d:
        off.append(b)
        b //= 2
    return off, d


def _row_broadcast_refs(cum, group, ref_offset):
    c, n = cum.shape
    parts = [jnp.broadcast_to(cum[p * group + ref_offset:p * group + ref_offset + 1, :], (group, n))
             for p in range(c // group)]
    return parts[0] if len(parts) == 1 else jnp.concatenate(parts, axis=0)


def _gla_kernel(*refs, c, nsub, nseq, dk, dv, has_s0):
    if has_s0:
        q_ref, k_ref, v_ref, gg_ref, la_ref, gn_ref, s0_ref, o_ref, sout_ref, s_scr = refs
    else:
        q_ref, k_ref, v_ref, gg_ref, la_ref, gn_ref, o_ref, sout_ref, s_scr = refs
    step = pl.program_id(1)

    @pl.when(step == 0)
    def _():
        s_scr[...] = s0_ref[...] if has_s0 else jnp.zeros_like(s_scr)

    for seq in range(nseq):
        for sub in range(nsub):
            first = (seq * nsub + sub) * c
            _gla_chunk(q_ref, k_ref, v_ref, gg_ref, la_ref, gn_ref, o_ref, s_scr.at[seq],
                       slice(first, first + c), c=c, dk=dk, dv=dv)

    @pl.when(step == pl.num_programs(1) - 1)
    def _():
        sout_ref[...] = s_scr[...]


def _gla_chunk(q_ref, k_ref, v_ref, gg_ref, la_ref, gn_ref, o_ref, s_scr, rows, *, c, dk, dv):
    row = lax.broadcasted_iota(jnp.int32, (c, c), 0)
    col = lax.broadcasted_iota(jnp.int32, (c, c), 1)
    tri = (col <= row).astype(F32)
    cum_all = jnp.dot(tri, la_ref[rows, :], precision=lax.Precision.HIGHEST, preferred_element_type=F32)

    off_levels, d = _gla_levels(c)
    scale = dk ** -0.5
    gn = gn_ref[...]

    for h in range(GLA_HEADS):
        ks = slice(h * dk, (h + 1) * dk)
        vs = slice(h * dv, (h + 1) * dv)
        cum = cum_all[:, ks]
        qs = q_ref[rows, ks].astype(F32) * scale
        kk = k_ref[rows, ks].astype(F32)
        vv = v_ref[rows, vs]

        ref = _row_broadcast_refs(cum, d, d // 2 - 1)
        qd = (qs * jnp.exp2(cum - ref)).astype(BF16)
        kd = (kk * jnp.exp2(ref - cum)).astype(BF16)
        att = lax.dot_general(qd, kd, NT_DIMS, preferred_element_type=F32)
        att = jnp.where((row // d == col // d) & (col <= row), att, 0.0)
        for b in off_levels:
            ref = _row_broadcast_refs(cum, 2 * b, b - 1)
            ql = (qs * jnp.exp2(jnp.minimum(cum - ref, 0.0))).astype(BF16)
            kl = (kk * jnp.exp2(jnp.minimum(ref - cum, 0.0))).astype(BF16)
            a_l = lax.dot_general(ql, kl, NT_DIMS, preferred_element_type=F32)
            att = jnp.where(((row // b) % 2 == 1) & (col // b == row // b - 1), a_l, att)

        s_h = s_scr[h]
        q_dec = (qs * jnp.exp2(cum)).astype(BF16)
        o = (jnp.dot(att.astype(BF16), vv, preferred_element_type=F32)
             + jnp.dot(q_dec, s_h.astype(BF16), preferred_element_type=F32))

        last = cum[c - 1:c, :]
        k_end = (kk * jnp.exp2(last - cum)).astype(BF16)
        kv = lax.dot_general(k_end, vv, TN_DIMS, preferred_element_type=F32)
        decay_col = jnp.transpose(jnp.broadcast_to(jnp.exp2(last), (dk, dk)))
        decay = jnp.concatenate([decay_col] * (dv // dk), axis=1)
        s_scr[h] = decay * s_h + kv

        ms = jnp.mean(o * o, -1, keepdims=True)
        gate = gg_ref[rows, vs]
        o_ref[rows, vs] = o * lax.rsqrt(ms + EPS) * gn * (gate * _sigmoid(gate))


def _gla(gq, gk, gv, gg, la, gn, s0, *, nb, t, c, nsub=1, nseq=1):
    rows, qk = gq.shape
    vd = gv.shape[1]
    dk, dv = qk // GLA_HEADS, vd // GLA_HEADS
    c = min(c, t)
    nsub = min(nsub, t // c)
    nc = t // (c * nsub)
    assert nseq == 1 or (nc == 1 and nb % nseq == 0), "several sequences per step only when a step spans them whole"
    blk = lambda n: pl.BlockSpec((nseq * c * nsub, n), lambda b, s: (b * nc + s, 0))
    state_spec = pl.BlockSpec((nseq, GLA_HEADS, dk, dv), lambda b, s: (b, 0, 0, 0))
    in_specs = [blk(qk), blk(qk), blk(vd), blk(vd), blk(qk), _resident(gn.shape)]
    args = [gq, gk, gv, gg, la, gn]
    if s0 is not None:
        in_specs.append(state_spec)
        args.append(s0)
    return pl.pallas_call(
        functools.partial(_gla_kernel, c=c, nsub=nsub, nseq=nseq, dk=dk, dv=dv, has_s0=s0 is not None),
        grid=(nb // nseq, nc),
        in_specs=in_specs,
        out_specs=[blk(vd), state_spec],
        out_shape=[jax.ShapeDtypeStruct((rows, vd), F32),
                   jax.ShapeDtypeStruct((nb, GLA_HEADS, dk, dv), F32)],
        scratch_shapes=[pltpu.VMEM((nseq, GLA_HEADS, dk, dv), F32)],
        compiler_params=_params(("parallel", "arbitrary")),
        name="gla",
    )(*args)


def _stack_half_queries(q):
    lane = lax.broadcasted_iota(jnp.int32, q.shape, 1)
    zero = jnp.zeros_like(q)
    return jnp.concatenate([jnp.where(lane < DIFF_HEAD_DIM, q, zero),
                            jnp.where(lane >= DIFF_HEAD_DIM, q, zero)], axis=0)


def _lambda(lq1_ref, lk1_ref, lq2_ref, lk2_ref, lam_init):
    return (jnp.exp(jnp.sum(lq1_ref[...] * lk1_ref[...], -1, keepdims=True))
            - jnp.exp(jnp.sum(lq2_ref[...] * lk2_ref[...], -1, keepdims=True)) + lam_init)


def _diff_finish(acc, l, lam, g, lam_init, tq):
    o = acc * (1.0 / l)
    od = o[:tq] - lam * o[tq:]
    ms = jnp.mean(od * od, -1, keepdims=True)
    return od * lax.rsqrt(ms + EPS) * g * (1.0 - lam_init)


def _attn_prompt_kernel(q_ref, k_ref, v_ref, lq1_ref, lk1_ref, lq2_ref, lk2_ref, g_ref, o_ref,
                        qs_scr, s_scr, p_scr, m_scr, l_scr, a_scr, acc_scr, *, tq, tk, lam_init):
    i = pl.program_id(2)
    q_first = i * tq
    qs_scr[...] = _stack_half_queries(q_ref[...])
    m_scr[...] = jnp.full(m_scr.shape, -jnp.inf, F32)
    l_scr[...] = jnp.zeros(l_scr.shape, F32)
    acc_scr[...] = jnp.zeros(acc_scr.shape, F32)
    reps = tk // LANES

    def tile(j, carry, diagonal):
        start = pl.multiple_of(j * tk, tk)
        s_scr[...] = jnp.dot(qs_scr[...], k_ref[:, pl.ds(start, tk)],
                             preferred_element_type=F32)
        for r0 in range(0, 2 * tq, CHUNK):
            rows = slice(r0, r0 + CHUNK)
            limit = ((r0 % tq) // CHUNK + 1) * CHUNK if diagonal else tk
            pieces = []
            for g in range(reps):
                lo = g * LANES
                if lo >= limit:
                    pieces.append(None)
                elif lo + LANES <= limit:
                    pieces.append(s_scr[rows, lo:lo + LANES])
                else:
                    lane = lax.broadcasted_iota(jnp.int32, (CHUNK, LANES), 1)
                    pieces.append(jnp.where(lane < limit - lo, s_scr[rows, lo:lo + LANES], -jnp.inf))
            seen = [x for x in pieces if x is not None]
            m_prev = m_scr[rows, :]
            m_next = jnp.maximum(m_prev, jnp.max(functools.reduce(jnp.maximum, seen), -1, keepdims=True))
            alpha = jnp.exp2(m_prev - m_next)
            p_lanes = None
            for g, x in enumerate(pieces):
                lo = g * LANES
                if x is None:
                    p_scr[rows, lo:lo + LANES] = jnp.zeros((CHUNK, LANES), BF16)
                    continue
                p = jnp.exp2(x - m_next)
                p_lanes = p if p_lanes is None else p_lanes + p
                p_scr[rows, lo:lo + LANES] = p.astype(BF16)
            l_scr[rows, :] = alpha * l_scr[rows, :] + p_lanes
            m_scr[rows, :] = m_next
            a_scr[rows, :] = alpha
        acc_scr[...] = a_scr[...] * acc_scr[...] + jnp.dot(p_scr[...], v_ref[pl.ds(start, tk), :],
                                                           preferred_element_type=F32)
        return carry

    n_full = q_first // tk
    def tile_group(jj, carry, first, size):
        for u in range(size):
            tile(first + size * jj + u, carry, diagonal=False)
        return carry

    done = 0
    for size in TILE_GROUPS:
        trips = (n_full - done) // size
        lax.fori_loop(0, trips, functools.partial(tile_group, first=done, size=size), 0)
        done = done + trips * size
    tile(n_full, 0, diagonal=True)

    lam = _lambda(lq1_ref, lk1_ref, lq2_ref, lk2_ref, lam_init)
    l = jnp.sum(l_scr[...], -1, keepdims=True)
    o_ref[...] = _diff_finish(acc_scr[...], l, lam, g_ref[...], lam_init, tq)


def _attn_prompt(dq, kb, vb, lams, g, *, nb, t, tile, lam_init):
    rows, dqk = dq.shape
    hw = 2 * DIFF_HEAD_DIM
    heads = dqk // hw
    tq = tk = min(tile, t)
    assert tq % CHUNK == 0 and t % tq == 0
    nq = t // tq
    k_spec = pl.BlockSpec((None, hw, t), lambda b, h, i: (b, h, 0))
    v_spec = pl.BlockSpec((t, hw), lambda b, h, i: (b, h))
    q_spec = pl.BlockSpec((tq, hw), lambda b, h, i: (b * nq + i, h))
    small = [_resident(a.shape) for a in (*lams, g)]
    stat = pltpu.VMEM((2 * tq, LANES), F32)
    return pl.pallas_call(
        functools.partial(_attn_prompt_kernel, tq=tq, tk=tk, lam_init=lam_init),
        grid=(nb, heads, nq),
        in_specs=[q_spec, k_spec, v_spec, *small],
        out_specs=q_spec,
        out_shape=jax.ShapeDtypeStruct((rows, dqk), F32),
        scratch_shapes=[pltpu.VMEM((2 * tq, hw), BF16), pltpu.VMEM((2 * tq, tk), F32),
                        pltpu.VMEM((2 * tq, tk), BF16), stat, stat, stat, pltpu.VMEM((2 * tq, hw), F32)],
        compiler_params=_params(("parallel", "parallel", "arbitrary")),
        name="attn_prompt",
    )(dq, kb, vb, *lams, g)


def _attn_sample_kernel(q_ref, kn_ref, vn_ref, kc_ref, vc_ref, lq1_ref, lk1_ref, lq2_ref, lk2_ref, g_ref,
                        o_ref, *, heads, lam_init):
    tq = q_ref.shape[0]
    hw = 2 * DIFF_HEAD_DIM
    past = kc_ref.shape[2]
    lam = _lambda(lq1_ref, lk1_ref, lq2_ref, lk2_ref, lam_init)
    g = g_ref[...]
    for h in range(heads):
        hs = slice(h * hw, (h + 1) * hw)
        qs = _stack_half_queries(q_ref[:, hs])
        k_past = kc_ref[0, hs, :].astype(BF16)
        v_past = vc_ref[0, pl.ds(h, past, stride=heads), :].astype(BF16)
        s_past = jnp.dot(qs, k_past, preferred_element_type=F32)
        s_new = lax.dot_general(qs, kn_ref[:, hs], NT_DIMS, preferred_element_type=F32)
        m = jnp.maximum(jnp.max(s_past, -1, keepdims=True), jnp.max(s_new, -1, keepdims=True))
        p_past = jnp.exp2(s_past - m)
        p_new = jnp.exp2(s_new - m)
        l = jnp.sum(p_past, -1, keepdims=True) + jnp.sum(p_new, -1, keepdims=True)
        acc = (jnp.dot(p_past.astype(BF16), v_past, preferred_element_type=F32)
               + jnp.dot(p_new.astype(BF16), vn_ref[:, hs], preferred_element_type=F32))
        o_ref[:, hs] = _diff_finish(acc, l, lam, g, lam_init, tq)


def _attn_sample(dq, kb, vb, k_cache, v_cache, lams, g, *, nb, t, lam_init):
    rows, dqk = dq.shape
    hw = 2 * DIFF_HEAD_DIM
    heads = dqk // hw
    past = k_cache.shape[2]
    q_pos = past + np.arange(t)
    assert ((q_pos // CHUNK + 1) * CHUNK >= past + t).all(), "sample queries must see every cached and new key"
    new_spec = pl.BlockSpec((t, dqk), lambda b: (b, 0))
    k_cache_spec = pl.BlockSpec((1, dqk, past), lambda b: (b, 0, 0))
    v_cache_spec = pl.BlockSpec((1, past * heads, hw), lambda b: (b, 0, 0))
    small = [_resident(a.shape) for a in (*lams, g)]
    return pl.pallas_call(
        functools.partial(_attn_sample_kernel, heads=heads, lam_init=lam_init),
        grid=(nb,),
        in_specs=[new_spec, new_spec, new_spec, k_cache_spec, v_cache_spec, *small],
        out_specs=new_spec,
        out_shape=jax.ShapeDtypeStruct((rows, dqk), F32),
        compiler_params=_params(("parallel",)),
        name="attn_sample",
    )(dq, kb, vb, k_cache, v_cache, *lams, g)


def _gelu_tanh(x):
    return x * (0.5 * (1.0 + jnp.tanh(math.sqrt(2.0 / math.pi) * (x + 0.044715 * (x * x * x)))))


def _ffn_kernel(*refs, tm, d_ff, fc, alpha, blocks_per_seq, seg):
    sample = seg is not None
    if sample:
        (x_ref, og_ref, od_ref, wg_ref, wo_ref, l1g_ref, l1b_ref, wup_ref, cw_ref, cb_ref, wdn_ref,
         l2g_ref, l2b_ref, past_ref, sel1_ref, sel2_ref, y_ref, a_ref,
         a_scr, x1_scr, x1b_scr, g_scr, h_scr, acc_scr) = refs
    else:
        (x_ref, og_ref, od_ref, wg_ref, wo_ref, l1g_ref, l1b_ref, wup_ref, cw_ref, cb_ref, wdn_ref,
         l2g_ref, l2b_ref, y_ref, a_ref,
         a_scr, x1_scr, x1b_scr, g_scr, h_scr, acc_scr) = refs
    d = x_ref.shape[1]
    pad = SUBLANES
    if sample:
        a_scr[0:pad, :] = jnp.zeros((pad, d_ff), F32)
        r = lax.broadcasted_iota(jnp.int32, (tm, fc), 0) % seg
    else:
        first = pl.program_id(0) % blocks_per_seq == 0

        @pl.when(first)
        def _():
            a_scr[0:pad, :] = jnp.zeros((pad, d_ff), F32)

        @pl.when(jnp.logical_not(first))
        def _():
            a_scr[0:pad, :] = a_scr[tm:tm + pad, :]

    xb_scr, mix_scr = x1b_scr, acc_scr
    xb_scr[...] = x_ref[...].astype(BF16)
    gate_a = jnp.dot(xb_scr[...], wg_ref[:, 0:d], preferred_element_type=F32)
    mix_scr[...] = _sigmoid(gate_a) * og_ref[...]
    gate_b = jnp.dot(xb_scr[...], wg_ref[:, d:2 * d], preferred_element_type=F32)
    xb_scr[...] = (mix_scr[...] + _sigmoid(gate_b) * od_ref[...]).astype(BF16)
    x1 = _layer_norm(alpha * x_ref[...] + jnp.dot(xb_scr[...], wo_ref[...], preferred_element_type=F32),
                     l1g_ref[...], l1b_ref[...])
    x1_scr[...] = x1
    x1b_scr[...] = x1.astype(BF16)

    def up(c):
        a_scr[pad:pad + tm, c * fc:(c + 1) * fc] = jnp.dot(
            x1b_scr[...], wup_ref[:, c * fc:(c + 1) * fc], preferred_element_type=F32)
        g_scr[c % 2] = jnp.dot(x1b_scr[...], wup_ref[:, d_ff + c * fc:d_ff + (c + 1) * fc],
                               preferred_element_type=F32)

    n_chunks = d_ff // fc
    up(0)
    for c in range(n_chunks):
        cs = slice(c * fc, (c + 1) * fc)
        if c + 1 < n_chunks:
            up(c + 1)
        a = a_scr[pad:pad + tm, cs]
        gate = g_scr[c % 2]
        a1 = a_scr[pad - 1:pad - 1 + tm, cs]
        a2 = a_scr[pad - 2:pad - 2 + tm, cs]
        if sample:
            past = past_ref[:, cs]
            hist1 = jnp.dot(sel1_ref[...], past, precision=lax.Precision.HIGHEST, preferred_element_type=F32)
            hist2 = jnp.dot(sel2_ref[...], past, precision=lax.Precision.HIGHEST, preferred_element_type=F32)
            a1 = jnp.where(r == 0, 0.0, a1) + hist1
            a2 = jnp.where(r < 2, 0.0, a2) + hist2
        conv = cb_ref[:, cs] + a2 * cw_ref[0:1, cs] + a1 * cw_ref[1:2, cs] + a * cw_ref[2:3, cs]
        h_scr[c % 2] = (_gelu_tanh(conv) * gate).astype(BF16)
        down = jnp.dot(h_scr[c % 2], wdn_ref[cs, :], preferred_element_type=F32)
        if c == 0:
            acc_scr[...] = down
        else:
            acc_scr[...] += down

    if sample:
        a_ref[...] = a_scr[pad:pad + tm, :]
    else:
        a_ref[0] = a_scr[tm:tm + pad, :]
    y_ref[...] = _layer_norm(alpha * x1_scr[...] + acc_scr[...], l2g_ref[...], l2b_ref[...])


def _ffn(x, og, od, w_gate, w_out, l1g, l1b, w_up, conv_w, conv_b, w_down, l2g, l2b,
         *, tm, alpha, seq_len, past=None, fc=FFN_CHUNK):
    rows, d = x.shape
    d_ff = w_down.shape[0]
    tm = min(tm, rows)
    nblk = rows // tm
    sample = past is not None
    row_spec = pl.BlockSpec((tm, d), lambda i: (i, 0))
    weights = [w_gate, w_out, l1g, l1b, w_up, conv_w, conv_b, w_down, l2g, l2b]
    args = [x, og, od, *weights]
    scratch = [pltpu.VMEM((tm + SUBLANES, d_ff), F32), pltpu.VMEM((tm, d), F32), pltpu.VMEM((tm, d), BF16),
               pltpu.VMEM((2, tm, fc), F32), pltpu.VMEM((2, tm, fc), BF16), pltpu.VMEM((tm, d), F32)]
    if sample:
        nseq = tm // seq_len
        sel1 = np.zeros((tm, past.shape[0]), np.float32)
        sel2 = np.zeros((tm, past.shape[0]), np.float32)
        for s in range(nseq):
            sel1[s * seq_len, 2 * s + 1] = 1.0
            sel2[s * seq_len, 2 * s] = 1.0
            sel2[s * seq_len + 1, 2 * s + 1] = 1.0
        assert nblk == 1
        args += [past, jnp.asarray(sel1), jnp.asarray(sel2)]
        a_shape = jax.ShapeDtypeStruct((rows, d_ff), F32)
        a_spec = pl.BlockSpec((tm, d_ff), lambda i: (i, 0))
        kern = functools.partial(_ffn_kernel, tm=tm, d_ff=d_ff, fc=fc, alpha=alpha, blocks_per_seq=None, seg=seq_len)
    else:
        a_shape = jax.ShapeDtypeStruct((nblk, SUBLANES, d_ff), F32)
        a_spec = pl.BlockSpec((1, SUBLANES, d_ff), lambda i: (i, 0, 0))
        kern = functools.partial(_ffn_kernel, tm=tm, d_ff=d_ff, fc=fc, alpha=alpha,
                                 blocks_per_seq=seq_len // tm, seg=None)
    in_specs = [row_spec, row_spec, row_spec] + [_resident(a.shape) for a in args[3:]]
    return pl.pallas_call(
        kern,
        grid=(nblk,),
        in_specs=in_specs,
        out_specs=[row_spec, a_spec],
        out_shape=[jax.ShapeDtypeStruct((rows, d), F32), a_shape],
        scratch_shapes=scratch,
        compiler_params=_params(("arbitrary",)),
        name="ffn",
    )(*args)


def kernel(x_prompt, x_sample, cache_diff_k, cache_diff_v, state_gla, cache_ffn_conv, w_in, w_a2, b_a, gla_norm_g, lam_q1, lam_k1, lam_q2, lam_k2, diff_norm_g, w_out, ln1_g, ln1_b, w_up, conv_w, conv_b, w_down, ln2_g, ln2_b):
    depth = w_in.shape[0]
    assert depth == 1, "single-layer step"
    nbp, tp, d = x_prompt.shape
    nbs, ts, _ = x_sample.shape
    past_len = cache_diff_k.shape[2]
    qk = w_a2.shape[2]
    vd = GLA_HEADS * gla_norm_g.shape[1]
    dqk = cache_diff_v.shape[3] * cache_diff_v.shape[4]
    d_ff = w_down.shape[1]
    alpha = (2.0 * depth) ** 0.25
    lam_init = 0.8 - 0.6 * math.exp(-0.3 * 0)

    w = w_in[0]
    o_low = 2 * qk + 2 * vd
    o_diff = o_low + GLA_RANK
    o_gate = o_diff + 3 * dqk
    w_gla = w[:, :o_low].astype(BF16)
    w_low = jnp.pad(w[:, o_low:o_diff], ((0, 0), (0, LANES - GLA_RANK))).astype(BF16)
    w_q = w[:, o_diff:o_diff + dqk].astype(BF16)
    w_k = w[:, o_diff + dqk:o_diff + 2 * dqk].astype(BF16)
    w_v = w[:, o_diff + 2 * dqk:o_gate].astype(BF16)
    w_gate = w[:, o_gate:].astype(BF16)
    w_a2p = jnp.pad(w_a2[0], ((0, LANES - GLA_RANK), (0, 0))).astype(BF16)
    row2 = lambda a: a.reshape(1, -1)
    lams = [row2(lam_q1[0]), row2(lam_k1[0]), row2(lam_q2[0]), row2(lam_k2[0])]
    ffn_w = (w_gate, w_out[0].astype(BF16), row2(ln1_g[0]), row2(ln1_b[0]), w_up[0].astype(BF16), conv_w[0],
             row2(conv_b[0]), w_down[0].astype(BF16), row2(ln2_g[0]), row2(ln2_b[0]))
    gn = row2(gla_norm_g[0])
    dg = row2(diff_norm_g[0])
    proj = functools.partial(_proj, w_gla=w_gla, w_low=w_low, w_a2=w_a2p, b_a=row2(b_a[0]), w_q=w_q, w_v=w_v,
                             qk=qk, vd=vd, dqk=dqk)
    heads = dqk // (2 * DIFF_HEAD_DIM)

    xp = x_prompt.reshape(nbp * tp, d)
    gq, gk, gv, gg, la, dq, kt_p, dv_p, kb, vb = proj(xp, w_k=w_k.T, tm=PROJ_ROWS, kt_frames=tp)
    dk_p = kt_p.reshape(1, nbp, heads, 2, DIFF_HEAD_DIM, tp).transpose(0, 1, 5, 2, 3, 4)
    og, state_p = _gla(gq, gk, gv, gg, la, gn, None, nb=nbp, t=tp, c=GLA_CHUNK, nsub=GLA_CHUNKS_PER_STEP)
    od = _attn_prompt(dq, kb, vb, lams, dg, nb=nbp, t=tp, tile=ATTN_TILE, lam_init=lam_init)
    y_p, a_last = _ffn(xp, og, od, *ffn_w, tm=FFN_ROWS, alpha=alpha, seq_len=tp)
    conv_p = a_last.reshape(nbp, -1, SUBLANES, d_ff)[:, -1, SUBLANES - (CONV_W - 1):, :]

    xs = x_sample.reshape(nbs * ts, d)
    gq, gk, gv, gg, la, dq, dk_s, dv_s, kb, vb = proj(xs, w_k=w_k, tm=PROJ_ROWS_SAMPLE)
    og, state_s = _gla(gq, gk, gv, gg, la, gn, state_gla[0], nb=nbs, t=ts, c=ts,
                       nseq=math.gcd(nbs, GLA_SAMPLE_SEQS_PER_STEP))
    k_cache_t = cache_diff_k[0].transpose(0, 2, 3, 4, 1).reshape(nbs, dqk, past_len)
    v_cache = cache_diff_v[0].reshape(nbs, past_len * heads, 2 * DIFF_HEAD_DIM)
    od = _attn_sample(dq, kb, vb, k_cache_t, v_cache, lams, dg, nb=nbs, t=ts, lam_init=lam_init)
    y_s, a_s = _ffn(xs, og, od, *ffn_w, tm=nbs * ts, alpha=alpha, seq_len=ts,
                    past=cache_ffn_conv[0].reshape(nbs * (CONV_W - 1), d_ff))
    conv_s = a_s.reshape(nbs, ts, d_ff)[:, ts - (CONV_W - 1):, :]

    v_shape = lambda nb, t: (1, nb, t, heads, 2 * DIFF_HEAD_DIM)
    return (y_p.reshape(nbp, tp, d), y_s.reshape(nbs, ts, d),
            dk_p, dv_p.reshape(v_shape(nbp, tp)), state_p[None], conv_p[None],
            dk_s.reshape(1, nbs, ts, heads, 2, DIFF_HEAD_DIM), dv_s.reshape(v_shape(nbs, ts)), state_s[None],
            conv_s[None])
```

```python
import functools
import math

import numpy as np
import jax
import jax.numpy as jnp
from jax import lax
from jax.experimental import pallas as pl
from jax.experimental.pallas import tpu as pltpu

F32 = jnp.float32
BF16 = jnp.bfloat16

CHUNK = 64
GLA_HEADS = 4
GLA_RANK = 16
GLA_TAU = 16.0
GLA_BLOCK = 16
DIFF_HEAD_DIM = 64
CONV_W = 3
EPS = 1e-5
LOG2_E = math.log2(math.e)

LANES = 128
SUBLANES = 8
VMEM_LIMIT = 56 * 1024 * 1024

PROJ_ROWS = 512
PROJ_ROWS_SAMPLE = 128
FFN_ROWS = 512
FFN_SUB = 256
FFN_CHUNK = 256
GLA_CHUNK = 128
GLA_CHUNKS_PER_STEP = 8
GLA_SAMPLE_SEQS_PER_STEP = 4
ATTN_TILE = 512
TILE_GROUPS = (8, 4, 2, 1)

NT_DIMS = (((1,), (1,)), ((), ()))
TN_DIMS = (((0,), (0,)), ((), ()))


def _sigmoid(x):
    return 1.0 / (1.0 + jnp.exp(-x))


def _layer_norm(x, g, b):
    mu = jnp.mean(x, -1, keepdims=True)
    xc = x - mu
    var = jnp.mean(xc * xc, -1, keepdims=True)
    return xc * lax.rsqrt(var + EPS) * g + b


def _resident(shape):
    nd = len(shape)
    return pl.BlockSpec(shape, lambda *_: (0,) * nd, pipeline_mode=pl.Buffered(1))


def _params(sem):
    return pltpu.CompilerParams(dimension_semantics=sem, vmem_limit_bytes=VMEM_LIMIT)


def _proj_kernel(x_ref, wgla_ref, wlow_ref, wa2_ref, ba_ref, wq_ref, wk_ref, wv_ref,
                 gq_ref, gk_ref, gv_ref, gg_ref, la_ref, dq_ref, dk_ref, dv_ref, kb_ref, vb_ref,
                 *, qk, vd, dqk, k_transposed):
    xb = x_ref[...].astype(BF16)

    def mm(w_ref, lo, hi):
        return jnp.dot(xb, w_ref[:, lo:hi], preferred_element_type=F32)

    gq_ref[...] = mm(wgla_ref, 0, qk)
    gk_ref[...] = mm(wgla_ref, qk, 2 * qk)
    gv_ref[...] = mm(wgla_ref, 2 * qk, 2 * qk + vd).astype(BF16)
    gg_ref[...] = mm(wgla_ref, 2 * qk + vd, 2 * qk + 2 * vd)

    g_low = jnp.dot(xb, wlow_ref[...], preferred_element_type=F32)
    z = jnp.dot(g_low.astype(BF16), wa2_ref[...], preferred_element_type=F32) + ba_ref[...]
    softplus_neg = jnp.maximum(-z, 0.0) + jnp.log1p(jnp.exp(-jnp.abs(z)))
    la_ref[...] = -softplus_neg * (LOG2_E / GLA_TAU)

    dq_ref[...] = (mm(wq_ref, 0, dqk) * (DIFF_HEAD_DIM ** -0.5 * LOG2_E)).astype(BF16)
    if k_transposed:
        dk = lax.dot_general(wk_ref[...], xb, NT_DIMS, preferred_element_type=F32)
        dk_ref[0] = dk
        kb_ref[0] = dk.astype(BF16)
    else:
        dk = mm(wk_ref, 0, dqk)
        dk_ref[...] = dk
        kb_ref[...] = dk.astype(BF16)
    dv = mm(wv_ref, 0, dqk)
    dv_ref[...] = dv
    vb_ref[...] = dv.astype(BF16)


def _proj(x, w_gla, w_low, w_a2, b_a, w_q, w_k, w_v, *, qk, vd, dqk, tm, kt_frames=None):
    rows, d = x.shape
    tm = min(tm, rows)
    row_spec = lambda n: pl.BlockSpec((tm, n), lambda i: (i, 0))
    out = lambda n, dt: jax.ShapeDtypeStruct((rows, n), dt)
    if kt_frames is None:
        k_spec, k_out = row_spec(dqk), lambda dt: out(dqk, dt)
    else:
        per_seq = kt_frames // tm
        k_spec = pl.BlockSpec((1, dqk, tm), lambda i: (i // per_seq, 0, i % per_seq))
        k_out = lambda dt: jax.ShapeDtypeStruct((rows // kt_frames, dqk, kt_frames), dt)
    weights = (w_gla, w_low, w_a2, b_a, w_q, w_k, w_v)
    return pl.pallas_call(
        functools.partial(_proj_kernel, qk=qk, vd=vd, dqk=dqk, k_transposed=kt_frames is not None),
        grid=(rows // tm,),
        in_specs=[row_spec(d)] + [_resident(a.shape) for a in weights],
        out_specs=[row_spec(qk), row_spec(qk), row_spec(vd), row_spec(vd), row_spec(qk),
                   row_spec(dqk), k_spec, row_spec(dqk), k_spec, row_spec(dqk)],
        out_shape=[out(qk, F32), out(qk, F32), out(vd, BF16), out(vd, F32), out(qk, F32),
                   out(dqk, BF16), k_out(F32), out(dqk, F32), k_out(BF16), out(dqk, BF16)],
        compiler_params=_params(("parallel",)),
        name="proj",
    )(x, *weights)


def _gla_levels(c):
    d = min(c, 2 * GLA_BLOCK)
    off = []
    b = c // 2
    while b >= d:
        off.append(b)
        b //= 2
    return off, d


def _row_broadcast_refs(cum, group, ref_offset):
    c, n = cum.shape
    parts = [jnp.broadcast_to(cum[p * group + ref_offset:p * group + ref_offset + 1, :], (group, n))
             for p in range(c // group)]
    return parts[0] if len(parts) == 1 else jnp.concatenate(parts, axis=0)


def _gla_kernel(*refs, c, nsub, nseq, dk, dv, has_s0):
    if has_s0:
        q_ref, k_ref, v_ref, gg_ref, la_ref, gn_ref, s0_ref, o_ref, sout_ref, s_scr = refs
    else:
        q_ref, k_ref, v_ref, gg_ref, la_ref, gn_ref, o_ref, sout_ref, s_scr = refs
    step = pl.program_id(1)

    @pl.when(step == 0)
    def _():
        s_scr[...] = s0_ref[...] if has_s0 else jnp.zeros_like(s_scr)

    for seq in range(nseq):
        for sub in range(nsub):
            first = (seq * nsub + sub) * c
            _gla_chunk(q_ref, k_ref, v_ref, gg_ref, la_ref, gn_ref, o_ref, s_scr.at[seq],
                       slice(first, first + c), c=c, dk=dk, dv=dv)

    @pl.when(step == pl.num_programs(1) - 1)
    def _():
        sout_ref[...] = s_scr[...]


def _gla_chunk(q_ref, k_ref, v_ref, gg_ref, la_ref, gn_ref, o_ref, s_scr, rows, *, c, dk, dv):
    row = lax.broadcasted_iota(jnp.int32, (c, c), 0)
    col = lax.broadcasted_iota(jnp.int32, (c, c), 1)
    tri = (col <= row).astype(F32)
    cum_all = jnp.dot(tri, la_ref[rows, :], precision=lax.Precision.HIGHEST, preferred_element_type=F32)

    off_levels, d = _gla_levels(c)
    scale = dk ** -0.5
    gn = gn_ref[...]

    for h in range(GLA_HEADS):
        ks = slice(h * dk, (h + 1) * dk)
        vs = slice(h * dv, (h + 1) * dv)
        cum = cum_all[:, ks]
        qs = q_ref[rows, ks] * scale
        kk = k_ref[rows, ks]
        vv = v_ref[rows, vs]

        ref = _row_broadcast_refs(cum, d, d // 2 - 1)
        qd = (qs * jnp.exp2(cum - ref)).astype(BF16)
        kd = (kk * jnp.exp2(ref - cum)).astype(BF16)
        att = lax.dot_general(qd, kd, NT_DIMS, preferred_element_type=F32)
        att = jnp.where((row // d == col // d) & (col <= row), att, 0.0)
        for b in off_levels:
            ref = _row_broadcast_refs(cum, 2 * b, b - 1)
            ql = (qs * jnp.exp2(jnp.minimum(cum - ref, 0.0))).astype(BF16)
            kl = (kk * jnp.exp2(jnp.minimum(ref - cum, 0.0))).astype(BF16)
            a_l = lax.dot_general(ql, kl, NT_DIMS, preferred_element_type=F32)
            att = jnp.where(((row // b) % 2 == 1) & (col // b == row // b - 1), a_l, att)

        s_h = s_scr[h]
        q_dec = (qs * jnp.exp2(cum)).astype(BF16)
        o = (jnp.dot(att.astype(BF16), vv, preferred_element_type=F32)
             + jnp.dot(q_dec, s_h.astype(BF16), preferred_element_type=F32))

        last = cum[c - 1:c, :]
        k_end = (kk * jnp.exp2(last - cum)).astype(BF16)
        kv = lax.dot_general(k_end, vv, TN_DIMS, preferred_element_type=F32)
        decay_col = jnp.transpose(jnp.broadcast_to(jnp.exp2(last), (dk, dk)))
        decay = jnp.concatenate([decay_col] * (dv // dk), axis=1)
        s_scr[h] = decay * s_h + kv

        ms = jnp.mean(o * o, -1, keepdims=True)
        gate = gg_ref[rows, vs]
        o_ref[rows, vs] = o * lax.rsqrt(ms + EPS) * gn * (gate * _sigmoid(gate))


def _gla(gq, gk, gv, gg, la, gn, s0, *, nb, t, c, nsub=1, nseq=1):
    rows, qk = gq.shape
    vd = gv.shape[1]
    dk, dv = qk // GLA_HEADS, vd // GLA_HEADS
    c = min(c, t)
    nsub = min(nsub, t // c)
    nc = t // (c * nsub)
    assert nseq == 1 or (nc == 1 and nb % nseq == 0), "several sequences per step only when a step spans them whole"
    blk = lambda n: pl.BlockSpec((nseq * c * nsub, n), lambda b, s: (b * nc + s, 0))
    state_spec = pl.BlockSpec((nseq, GLA_HEADS, dk, dv), lambda b, s: (b, 0, 0, 0))
    in_specs = [blk(qk), blk(qk), blk(vd), blk(vd), blk(qk), _resident(gn.shape)]
    args = [gq, gk, gv, gg, la, gn]
    if s0 is not None:
        in_specs.append(state_spec)
        args.append(s0)
    return pl.pallas_call(
        functools.partial(_gla_kernel, c=c, nsub=nsub, nseq=nseq, dk=dk, dv=dv, has_s0=s0 is not None),
        grid=(nb // nseq, nc),
        in_specs=in_specs,
        out_specs=[blk(vd), state_spec],
        out_shape=[jax.ShapeDtypeStruct((rows, vd), F32),
                   jax.ShapeDtypeStruct((nb, GLA_HEADS, dk, dv), F32)],
        scratch_shapes=[pltpu.VMEM((nseq, GLA_HEADS, dk, dv), F32)],
        compiler_params=_params(("parallel", "arbitrary")),
        name="gla",
    )(*args)


def _stack_half_queries(q):
    lane = lax.broadcasted_iota(jnp.int32, q.shape, 1)
    zero = jnp.zeros_like(q)
    return jnp.concatenate([jnp.where(lane < DIFF_HEAD_DIM, q, zero),
                            jnp.where(lane >= DIFF_HEAD_DIM, q, zero)], axis=0)


def _lambda(lq1_ref, lk1_ref, lq2_ref, lk2_ref, lam_init):
    return (jnp.exp(jnp.sum(lq1_ref[...] * lk1_ref[...], -1, keepdims=True))
            - jnp.exp(jnp.sum(lq2_ref[...] * lk2_ref[...], -1, keepdims=True)) + lam_init)


def _diff_finish(acc, l, lam, g, lam_init, tq):
    o = acc * (1.0 / l)
    od = o[:tq] - lam * o[tq:]
    ms = jnp.mean(od * od, -1, keepdims=True)
    return od * lax.rsqrt(ms + EPS) * g * (1.0 - lam_init)


def _attn_prompt_kernel(q_ref, k_ref, v_ref, lq1_ref, lk1_ref, lq2_ref, lk2_ref, g_ref, o_ref,
                        qs_scr, s_scr, p_scr, m_scr, l_scr, a_scr, acc_scr, *, tq, tk, lam_init):
    i = pl.program_id(2)
    q_first = i * tq
    qs_scr[...] = _stack_half_queries(q_ref[...])
    m_scr[...] = jnp.full(m_scr.shape, -jnp.inf, F32)
    l_scr[...] = jnp.zeros(l_scr.shape, F32)
    acc_scr[...] = jnp.zeros(acc_scr.shape, F32)
    reps = tk // LANES

    def tile(j, carry, diagonal):
        start = pl.multiple_of(j * tk, tk)
        s_scr[...] = jnp.dot(qs_scr[...], k_ref[:, pl.ds(start, tk)],
                             preferred_element_type=F32)
        for r0 in range(0, 2 * tq, CHUNK):
            rows = slice(r0, r0 + CHUNK)
            limit = ((r0 % tq) // CHUNK + 1) * CHUNK if diagonal else tk
            pieces = []
            for g in range(reps):
                lo = g * LANES
                if lo >= limit:
                    pieces.append(None)
                elif lo + LANES <= limit:
                    pieces.append(s_scr[rows, lo:lo + LANES])
                else:
                    lane = lax.broadcasted_iota(jnp.int32, (CHUNK, LANES), 1)
                    pieces.append(jnp.where(lane < limit - lo, s_scr[rows, lo:lo + LANES], -jnp.inf))
            seen = [x for x in pieces if x is not None]
            m_prev = m_scr[rows, :]
            m_next = jnp.maximum(m_prev, jnp.max(functools.reduce(jnp.maximum, seen), -1, keepdims=True))
            alpha = jnp.exp2(m_prev - m_next)
            p_lanes = None
            for g, x in enumerate(pieces):
                lo = g * LANES
                if x is None:
                    p_scr[rows, lo:lo + LANES] = jnp.zeros((CHUNK, LANES), BF16)
                    continue
                p = jnp.exp2(x - m_next)
                p_lanes = p if p_lanes is None else p_lanes + p
                p_scr[rows, lo:lo + LANES] = p.astype(BF16)
            l_scr[rows, :] = alpha * l_scr[rows, :] + p_lanes
            m_scr[rows, :] = m_next
            a_scr[rows, :] = alpha
        acc_scr[...] = a_scr[...] * acc_scr[...] + jnp.dot(p_scr[...], v_ref[pl.ds(start, tk), :],
                                                           preferred_element_type=F32)
        return carry

    n_full = q_first // tk
    def tile_group(jj, carry, first, size):
        for u in range(size):
            tile(first + size * jj + u, carry, diagonal=False)
        return carry

    done = 0
    for size in TILE_GROUPS:
        trips = (n_full - done) // size
        lax.fori_loop(0, trips, functools.partial(tile_group, first=done, size=size), 0)
        done = done + trips * size
    tile(n_full, 0, diagonal=True)

    lam = _lambda(lq1_ref, lk1_ref, lq2_ref, lk2_ref, lam_init)
    l = jnp.sum(l_scr[...], -1, keepdims=True)
    o_ref[...] = _diff_finish(acc_scr[...], l, lam, g_ref[...], lam_init, tq)


def _attn_prompt(dq, kb, vb, lams, g, *, nb, t, tile, lam_init):
    rows, dqk = dq.shape
    hw = 2 * DIFF_HEAD_DIM
    heads = dqk // hw
    tq = tk = min(tile, t)
    assert tq % CHUNK == 0 and t % tq == 0
    nq = t // tq
    k_spec = pl.BlockSpec((None, hw, t), lambda b, h, i: (b, h, 0))
    v_spec = pl.BlockSpec((t, hw), lambda b, h, i: (b, h))
    q_spec = pl.BlockSpec((tq, hw), lambda b, h, i: (b * nq + i, h))
    small = [_resident(a.shape) for a in (*lams, g)]
    stat = pltpu.VMEM((2 * tq, LANES), F32)
    return pl.pallas_call(
        functools.partial(_attn_prompt_kernel, tq=tq, tk=tk, lam_init=lam_init),
        grid=(nb, heads, nq),
        in_specs=[q_spec, k_spec, v_spec, *small],
        out_specs=q_spec,
        out_shape=jax.ShapeDtypeStruct((rows, dqk), F32),
        scratch_shapes=[pltpu.VMEM((2 * tq, hw), BF16), pltpu.VMEM((2 * tq, tk), F32),
                        pltpu.VMEM((2 * tq, tk), BF16), stat, stat, stat, pltpu.VMEM((2 * tq, hw), F32)],
        compiler_params=_params(("parallel", "parallel", "arbitrary")),
        name="attn_prompt",
    )(dq, kb, vb, *lams, g)


def _attn_sample_kernel(q_ref, kn_ref, vn_ref, kc_ref, vc_ref, lq1_ref, lk1_ref, lq2_ref, lk2_ref, g_ref,
                        o_ref, *, heads, lam_init):
    tq = q_ref.shape[0]
    hw = 2 * DIFF_HEAD_DIM
    past = kc_ref.shape[2]
    lam = _lambda(lq1_ref, lk1_ref, lq2_ref, lk2_ref, lam_init)
    g = g_ref[...]
    for h in range(heads):
        hs = slice(h * hw, (h + 1) * hw)
        qs = _stack_half_queries(q_ref[:, hs])
        k_past = kc_ref[0, hs, :].astype(BF16)
        v_past = vc_ref[0, pl.ds(h, past, stride=heads), :].astype(BF16)
        s_past = jnp.dot(qs, k_past, preferred_element_type=F32)
        s_new = lax.dot_general(qs, kn_ref[:, hs], NT_DIMS, preferred_element_type=F32)
        m = jnp.maximum(jnp.max(s_past, -1, keepdims=True), jnp.max(s_new, -1, keepdims=True))
        p_past = jnp.exp2(s_past - m)
        p_new = jnp.exp2(s_new - m)
        l = jnp.sum(p_past, -1, keepdims=True) + jnp.sum(p_new, -1, keepdims=True)
        acc = (jnp.dot(p_past.astype(BF16), v_past, preferred_element_type=F32)
               + jnp.dot(p_new.astype(BF16), vn_ref[:, hs], preferred_element_type=F32))
        o_ref[:, hs] = _diff_finish(acc, l, lam, g, lam_init, tq)


def _attn_sample(dq, kb, vb, k_cache, v_cache, lams, g, *, nb, t, lam_init):
    rows, dqk = dq.shape
    hw = 2 * DIFF_HEAD_DIM
    heads = dqk // hw
    past = k_cache.shape[2]
    q_pos = past + np.arange(t)
    assert ((q_pos // CHUNK + 1) * CHUNK >= past + t).all(), "sample queries must see every cached and new key"
    new_spec = pl.BlockSpec((t, dqk), lambda b: (b, 0))
    k_cache_spec = pl.BlockSpec((1, dqk, past), lambda b: (b, 0, 0))
    v_cache_spec = pl.BlockSpec((1, past * heads, hw), lambda b: (b, 0, 0))
    small = [_resident(a.shape) for a in (*lams, g)]
    return pl.pallas_call(
        functools.partial(_attn_sample_kernel, heads=heads, lam_init=lam_init),
        grid=(nb,),
        in_specs=[new_spec, new_spec, new_spec, k_cache_spec, v_cache_spec, *small],
        out_specs=new_spec,
        out_shape=jax.ShapeDtypeStruct((rows, dqk), F32),
        compiler_params=_params(("parallel",)),
        name="attn_sample",
    )(dq, kb, vb, k_cache, v_cache, *lams, g)


def _gelu_tanh(x):
    return x * (0.5 * (1.0 + jnp.tanh(math.sqrt(2.0 / math.pi) * (x + 0.044715 * (x * x * x)))))


def _ffn_kernel(*refs, tm, d_ff, fc, alpha, blocks_per_seq, seg):
    sample = seg is not None
    if sample:
        (x_ref, og_ref, od_ref, wg_ref, wo_ref, l1g_ref, l1b_ref, wup_ref, cw_ref, cb_ref, wdn_ref,
         l2g_ref, l2b_ref, past_ref, sel1_ref, sel2_ref, y_ref, a_ref,
         a_scr, x1_scr, x1b_scr, g_scr, h_scr, acc_scr) = refs
    else:
        (x_ref, og_ref, od_ref, wg_ref, wo_ref, l1g_ref, l1b_ref, wup_ref, cw_ref, cb_ref, wdn_ref,
         l2g_ref, l2b_ref, y_ref, a_ref,
         a_scr, x1_scr, x1b_scr, g_scr, h_scr, acc_scr) = refs
    d = x_ref.shape[1]
    pad = SUBLANES
    if sample:
        a_scr[0:pad, :] = jnp.zeros((pad, d_ff), F32)
    else:
        first = pl.program_id(0) % blocks_per_seq == 0

        @pl.when(first)
        def _():
            a_scr[0:pad, :] = jnp.zeros((pad, d_ff), F32)

        @pl.when(jnp.logical_not(first))
        def _():
            a_scr[0:pad, :] = a_scr[tm:tm + pad, :]

    subs = [slice(r0, min(r0 + FFN_SUB, tm)) for r0 in range(0, tm, FFN_SUB)]
    xb_scr, mix_scr = x1b_scr, acc_scr
    for h in subs:
        xb_scr[h, :] = x_ref[h, :].astype(BF16)
    gate_a = jnp.dot(xb_scr[...], wg_ref[:, 0:d], preferred_element_type=F32)
    for h in subs:
        mix_scr[h, :] = _sigmoid(gate_a[h]) * og_ref[h, :]
    gate_b = jnp.dot(xb_scr[...], wg_ref[:, d:2 * d], preferred_element_type=F32)
    for h in subs:
        xb_scr[h, :] = (mix_scr[h, :] + _sigmoid(gate_b[h]) * od_ref[h, :]).astype(BF16)
    mixed_out = jnp.dot(xb_scr[...], wo_ref[...], preferred_element_type=F32)
    for h in subs:
        x1 = _layer_norm(alpha * x_ref[h, :] + mixed_out[h], l1g_ref[...], l1b_ref[...])
        x1_scr[h, :] = x1
        x1b_scr[h, :] = x1.astype(BF16)

    def up(c):
        a_scr[pad:pad + tm, c * fc:(c + 1) * fc] = jnp.dot(
            x1b_scr[...], wup_ref[:, c * fc:(c + 1) * fc], preferred_element_type=F32)
        g_scr[c % 2] = jnp.dot(x1b_scr[...], wup_ref[:, d_ff + c * fc:d_ff + (c + 1) * fc],
                               preferred_element_type=F32)

    n_chunks = d_ff // fc
    up(0)
    for c in range(n_chunks):
        cs = slice(c * fc, (c + 1) * fc)
        if c + 1 < n_chunks:
            up(c + 1)
        if sample:
            past = past_ref[:, cs]
            hist1 = jnp.dot(sel1_ref[...], past, precision=lax.Precision.HIGHEST, preferred_element_type=F32)
            hist2 = jnp.dot(sel2_ref[...], past, precision=lax.Precision.HIGHEST, preferred_element_type=F32)
        for h in subs:
            lo, hi = h.start, h.stop
            a = a_scr[pad + lo:pad + hi, cs]
            a1 = a_scr[pad - 1 + lo:pad - 1 + hi, cs]
            a2 = a_scr[pad - 2 + lo:pad - 2 + hi, cs]
            if sample:
                r = (lo + lax.broadcasted_iota(jnp.int32, (hi - lo, fc), 0)) % seg
                a1 = jnp.where(r == 0, 0.0, a1) + hist1[h]
                a2 = jnp.where(r < 2, 0.0, a2) + hist2[h]
            conv = cb_ref[:, cs] + a2 * cw_ref[0:1, cs] + a1 * cw_ref[1:2, cs] + a * cw_ref[2:3, cs]
            h_scr[c % 2, h, :] = (_gelu_tanh(conv) * g_scr[c % 2, h, :]).astype(BF16)
        down = jnp.dot(h_scr[c % 2], wdn_ref[cs, :], preferred_element_type=F32)
        if c == 0:
            acc_scr[...] = down
        else:
            acc_scr[...] += down

    if sample:
        a_ref[...] = a_scr[pad:pad + tm, :]
    else:
        a_ref[0] = a_scr[tm:tm + pad, :]
    for h in subs:
        y_ref[h, :] = _layer_norm(alpha * x1_scr[h, :] + acc_scr[h, :], l2g_ref[...], l2b_ref[...])


def _ffn(x, og, od, w_gate, w_out, l1g, l1b, w_up, conv_w, conv_b, w_down, l2g, l2b,
         *, tm, alpha, seq_len, past=None, fc=FFN_CHUNK):
    rows, d = x.shape
    d_ff = w_down.shape[0]
    tm = min(tm, rows)
    nblk = rows // tm
    sample = past is not None
    row_spec = pl.BlockSpec((tm, d), lambda i: (i, 0))
    weights = [w_gate, w_out, l1g, l1b, w_up, conv_w, conv_b, w_down, l2g, l2b]
    args = [x, og, od, *weights]
    scratch = [pltpu.VMEM((tm + SUBLANES, d_ff), F32), pltpu.VMEM((tm, d), F32), pltpu.VMEM((tm, d), BF16),
               pltpu.VMEM((2, tm, fc), F32), pltpu.VMEM((2, tm, fc), BF16), pltpu.VMEM((tm, d), F32)]
    if sample:
        nseq = tm // seq_len
        sel1 = np.zeros((tm, past.shape[0]), np.float32)
        sel2 = np.zeros((tm, past.shape[0]), np.float32)
        for s in range(nseq):
            sel1[s * seq_len, 2 * s + 1] = 1.0
            sel2[s * seq_len, 2 * s] = 1.0
            sel2[s * seq_len + 1, 2 * s + 1] = 1.0
        assert nblk == 1
        args += [past, jnp.asarray(sel1), jnp.asarray(sel2)]
        a_shape = jax.ShapeDtypeStruct((rows, d_ff), F32)
        a_spec = pl.BlockSpec((tm, d_ff), lambda i: (i, 0))
        kern = functools.partial(_ffn_kernel, tm=tm, d_ff=d_ff, fc=fc, alpha=alpha, blocks_per_seq=None, seg=seq_len)
    else:
        a_shape = jax.ShapeDtypeStruct((nblk, SUBLANES, d_ff), F32)
        a_spec = pl.BlockSpec((1, SUBLANES, d_ff), lambda i: (i, 0, 0))
        kern = functools.partial(_ffn_kernel, tm=tm, d_ff=d_ff, fc=fc, alpha=alpha,
                                 blocks_per_seq=seq_len // tm, seg=None)
    in_specs = [row_spec, row_spec, row_spec] + [_resident(a.shape) for a in args[3:]]
    return pl.pallas_call(
        kern,
        grid=(nblk,),
        in_specs=in_specs,
        out_specs=[row_spec, a_spec],
        out_shape=[jax.ShapeDtypeStruct((rows, d), F32), a_shape],
        scratch_shapes=scratch,
        compiler_params=_params(("arbitrary",)),
        name="ffn",
    )(*args)


def kernel(x_prompt, x_sample, cache_diff_k, cache_diff_v, state_gla, cache_ffn_conv, w_in, w_a2, b_a, gla_norm_g, lam_q1, lam_k1, lam_q2, lam_k2, diff_norm_g, w_out, ln1_g, ln1_b, w_up, conv_w, conv_b, w_down, ln2_g, ln2_b):
    depth = w_in.shape[0]
    assert depth == 1, "single-layer step"
    nbp, tp, d = x_prompt.shape
    nbs, ts, _ = x_sample.shape
    past_len = cache_diff_k.shape[2]
    qk = w_a2.shape[2]
    vd = GLA_HEADS * gla_norm_g.shape[1]
    dqk = cache_diff_v.shape[3] * cache_diff_v.shape[4]
    d_ff = w_down.shape[1]
    alpha = (2.0 * depth) ** 0.25
    lam_init = 0.8 - 0.6 * math.exp(-0.3 * 0)

    w = w_in[0]
    o_low = 2 * qk + 2 * vd
    o_diff = o_low + GLA_RANK
    o_gate = o_diff + 3 * dqk
    w_gla = w[:, :o_low].astype(BF16)
    w_low = jnp.pad(w[:, o_low:o_diff], ((0, 0), (0, LANES - GLA_RANK))).astype(BF16)
    w_q = w[:, o_diff:o_diff + dqk].astype(BF16)
    w_k = w[:, o_diff + dqk:o_diff + 2 * dqk].astype(BF16)
    w_v = w[:, o_diff + 2 * dqk:o_gate].astype(BF16)
    w_gate = w[:, o_gate:].astype(BF16)
    w_a2p = jnp.pad(w_a2[0], ((0, LANES - GLA_RANK), (0, 0))).astype(BF16)
    row2 = lambda a: a.reshape(1, -1)
    lams = [row2(lam_q1[0]), row2(lam_k1[0]), row2(lam_q2[0]), row2(lam_k2[0])]
    ffn_w = (w_gate, w_out[0].astype(BF16), row2(ln1_g[0]), row2(ln1_b[0]), w_up[0].astype(BF16), conv_w[0],
             row2(conv_b[0]), w_down[0].astype(BF16), row2(ln2_g[0]), row2(ln2_b[0]))
    gn = row2(gla_norm_g[0])
    dg = row2(diff_norm_g[0])
    proj = functools.partial(_proj, w_gla=w_gla, w_low=w_low, w_a2=w_a2p, b_a=row2(b_a[0]), w_q=w_q, w_v=w_v,
                             qk=qk, vd=vd, dqk=dqk)
    heads = dqk // (2 * DIFF_HEAD_DIM)

    xp = x_prompt.reshape(nbp * tp, d)
    gq, gk, gv, gg, la, dq, kt_p, dv_p, kb, vb = proj(xp, w_k=w_k.T, tm=PROJ_ROWS, kt_frames=tp)
    dk_p = kt_p.reshape(1, nbp, heads, 2, DIFF_HEAD_DIM, tp).transpose(0, 1, 5, 2, 3, 4)
    og, state_p = _gla(gq, gk, gv, gg, la, gn, None, nb=nbp, t=tp, c=GLA_CHUNK, nsub=GLA_CHUNKS_PER_STEP)
    od = _attn_prompt(dq, kb, vb, lams, dg, nb=nbp, t=tp, tile=ATTN_TILE, lam_init=lam_init)
    y_p, a_last = _ffn(xp, og, od, *ffn_w, tm=FFN_ROWS, alpha=alpha, seq_len=tp)
    conv_p = a_last.reshape(nbp, -1, SUBLANES, d_ff)[:, -1, SUBLANES - (CONV_W - 1):, :]

    xs = x_sample.reshape(nbs * ts, d)
    gq, gk, gv, gg, la, dq, dk_s, dv_s, kb, vb = proj(xs, w_k=w_k, tm=PROJ_ROWS_SAMPLE)
    og, state_s = _gla(gq, gk, gv, gg, la, gn, state_gla[0], nb=nbs, t=ts, c=ts,
                       nseq=math.gcd(nbs, GLA_SAMPLE_SEQS_PER_STEP))
    k_cache_t = cache_diff_k[0].transpose(0, 2, 3, 4, 1).reshape(nbs, dqk, past_len)
    v_cache = cache_diff_v[0].reshape(nbs, past_len * heads, 2 * DIFF_HEAD_DIM)
    od = _attn_sample(dq, kb, vb, k_cache_t, v_cache, lams, dg, nb=nbs, t=ts, lam_init=lam_init)
    y_s, a_s = _ffn(xs, og, od, *ffn_w, tm=nbs * ts, alpha=alpha, seq_len=ts,
                    past=cache_ffn_conv[0].reshape(nbs * (CONV_W - 1), d_ff))
    conv_s = a_s.reshape(nbs, ts, d_ff)[:, ts - (CONV_W - 1):, :]

    v_shape = lambda nb, t: (1, nb, t, heads, 2 * DIFF_HEAD_DIM)
    return (y_p.reshape(nbp, tp, d), y_s.reshape(nbs, ts, d),
            dk_p, dv_p.reshape(v_shape(nbp, tp)), state_p[None], conv_p[None],
            dk_s.reshape(1, nbs, ts, heads, 2, DIFF_HEAD_DIM), dv_s.reshape(v_shape(nbs, ts)), state_s[None],
            conv_s[None])
```
